```python
import math
import jax, jax.numpy as jnp
from jax import lax
import numpy as np

D_MODEL = 1024
BATCH = 8
SEQ = 8192
DEPTH = 1

N_HEADS = 8
HEAD_DIM = 64
V_HEAD_DIM = 2 * HEAD_DIM
ATTN_WIDTH = N_HEADS * V_HEAD_DIM
QK_WIDTH = N_HEADS * 2 * HEAD_DIM
CONV_WIDTH = D_MODEL
SHORT_CONV_K = 3
FFN_CONV_K = 3
D_FF = 2816
ROPE_THETA = 10000.0
Q_BLOCK = 128
NORM_EPS = 1e-6
SPLIT_SIZES = (QK_WIDTH, QK_WIDTH, ATTN_WIDTH, CONV_WIDTH, CONV_WIDTH, CONV_WIDTH, D_MODEL, D_MODEL)
SPLIT_POINTS = tuple(int(s) for s in np.cumsum(SPLIT_SIZES)[:-1])
D_IN = int(sum(SPLIT_SIZES))

kernel_name = 'hybrid_diffattn_shortconv_gated_block'


def rms_norm(x, g):
    xf = x.astype(jnp.float32)
    y = xf * lax.rsqrt(jnp.mean(xf * xf, axis=-1, keepdims=True) + NORM_EPS)
    return (y * g.astype(jnp.float32)).astype(x.dtype)


def causal_dwconv(x, w, b=None):
    k_width, chans = w.shape
    y = lax.conv_general_dilated(
        x, w[:, None, :].astype(x.dtype), window_strides=(1,),
        padding=((k_width - 1, 0),), dimension_numbers=('NWC', 'WIO', 'NWC'),
        feature_group_count=chans)
    if b is not None:
        y = y + b.astype(x.dtype)
    return y


def rope_tables(seq, dim):
    inv = ROPE_THETA ** (-jnp.arange(0, dim, 2, dtype=jnp.float32) / dim)
    ang = jnp.arange(seq, dtype=jnp.float32)[:, None] * inv[None, :]
    return jnp.cos(ang), jnp.sin(ang)


def apply_rope(x, cos, sin):
    half = x.shape[-1] // 2
    c = cos[:, None, None, :].astype(x.dtype)
    s = sin[:, None, None, :].astype(x.dtype)
    x1, x2 = x[..., :half], x[..., half:]
    return jnp.concatenate([x1 * c - x2 * s, x2 * c + x1 * s], axis=-1)


def diff_attention(q, k, v, lam):
    bsz, seq = q.shape[0], q.shape[1]
    n_blk = seq // Q_BLOCK
    scale = HEAD_DIM ** -0.5
    qb = q.reshape(bsz, n_blk, Q_BLOCK, N_HEADS, 2, HEAD_DIM).swapaxes(0, 1)
    starts = jnp.arange(n_blk, dtype=jnp.int32) * Q_BLOCK
    kpos = jnp.arange(seq, dtype=jnp.int32)

    def one_block(args):
        q_blk, s0 = args
        s = jnp.einsum('bqhmd,bkhmd->bhmqk', q_blk, k).astype(jnp.float32) * scale
        qpos = s0 + jnp.arange(Q_BLOCK, dtype=jnp.int32)
        causal = kpos[None, :] <= qpos[:, None]
        s = jnp.where(causal, s, -jnp.inf)
        p = jax.nn.softmax(s, axis=-1)
        a = p[:, :, 0] - lam.astype(jnp.float32) * p[:, :, 1]
        return jnp.einsum('bhqk,bkhe->bqhe', a.astype(v.dtype), v)

    out = lax.map(one_block, (qb, starts))
    return out.swapaxes(0, 1).reshape(bsz, seq, N_HEADS, V_HEAD_DIM)


def setup_inputs(seed: int = 0) -> dict:
    key = jax.random.key(seed)
    ks = jax.random.split(key, 20)
    f32 = jnp.float32
    nrm = lambda k, shape, s: jax.random.normal(k, shape, f32) * s
    return {
        'x': nrm(ks[0], (BATCH, SEQ, D_MODEL), 1.0),
        'attn_norm_g': 1.0 + nrm(ks[1], (DEPTH, D_MODEL), 0.02),
        'w_in': nrm(ks[2], (DEPTH, D_MODEL, D_IN), D_MODEL ** -0.5),
        'b_gate': nrm(ks[3], (DEPTH, 2 * D_MODEL), 0.02),
        'q_norm_g': 1.0 + nrm(ks[4], (DEPTH, HEAD_DIM), 0.02),
        'k_norm_g': 1.0 + nrm(ks[5], (DEPTH, HEAD_DIM), 0.02),
        'lambda_q1': nrm(ks[6], (DEPTH, HEAD_DIM), 0.1),
        'lambda_k1': nrm(ks[7], (DEPTH, HEAD_DIM), 0.1),
        'lambda_q2': nrm(ks[8], (DEPTH, HEAD_DIM), 0.1),
        'lambda_k2': nrm(ks[9], (DEPTH, HEAD_DIM), 0.1),
        'subln_g': 1.0 + nrm(ks[10], (DEPTH, V_HEAD_DIM), 0.02),
        'short_conv_w': nrm(ks[11], (DEPTH, SHORT_CONV_K, CONV_WIDTH), SHORT_CONV_K ** -0.5),
        'w_out': nrm(ks[12], (DEPTH, D_MODEL, D_MODEL), D_MODEL ** -0.5),
        'ffn_norm_g': 1.0 + nrm(ks[13], (DEPTH, D_MODEL), 0.02),
        'w_up': nrm(ks[14], (DEPTH, D_MODEL, 2 * D_FF), D_MODEL ** -0.5),
        'ffn_conv_w': nrm(ks[15], (DEPTH, FFN_CONV_K, 2 * D_FF), FFN_CONV_K ** -0.5),
        'ffn_conv_b': nrm(ks[16], (DEPTH, 2 * D_FF), 0.02),
        'w_down': nrm(ks[17], (DEPTH, D_FF, D_MODEL), D_FF ** -0.5),
    }


def reference(x, attn_norm_g, w_in, b_gate, q_norm_g, k_norm_g, lambda_q1, lambda_k1,
              lambda_q2, lambda_k2, subln_g, short_conv_w, w_out, ffn_norm_g, w_up,
              ffn_conv_w, ffn_conv_b, w_down):
    bsz, seq = x.shape[0], x.shape[1]
    cos, sin = rope_tables(seq, HEAD_DIM)
    for l in range(DEPTH):
        lambda_init = 0.8 - 0.6 * math.exp(-0.3 * l)
        h = rms_norm(x, attn_norm_g[l])
        proj = h @ w_in[l]
        q, k, v, c_b, c_c, c_x, g_a, g_c = jnp.split(proj, SPLIT_POINTS, axis=-1)
        q = q.reshape(bsz, seq, N_HEADS, 2, HEAD_DIM)
        k = k.reshape(bsz, seq, N_HEADS, 2, HEAD_DIM)
        v = v.reshape(bsz, seq, N_HEADS, V_HEAD_DIM)
        q = apply_rope(rms_norm(q, q_norm_g[l]), cos, sin)
        k = apply_rope(rms_norm(k, k_norm_g[l]), cos, sin)
        lam = (jnp.exp(jnp.sum(lambda_q1[l].astype(jnp.float32) * lambda_k1[l].astype(jnp.float32)))
               - jnp.exp(jnp.sum(lambda_q2[l].astype(jnp.float32) * lambda_k2[l].astype(jnp.float32)))
               + lambda_init)
        o = diff_attention(q, k, v, lam)
        o = rms_norm(o, subln_g[l]) * (1.0 - lambda_init)
        y_attn = o.reshape(bsz, seq, ATTN_WIDTH)
        y_conv = c_b * causal_dwconv(c_c * c_x, short_conv_w[l])
        gate_a = jax.nn.sigmoid(g_a + b_gate[l, :D_MODEL])
        gate_c = jax.nn.sigmoid(g_c + b_gate[l, D_MODEL:])
        merged = gate_a * y_attn + gate_c * y_conv
        x = x + merged @ w_out[l]
        h = rms_norm(x, ffn_norm_g[l])
        u = causal_dwconv(h @ w_up[l], ffn_conv_w[l], ffn_conv_b[l])
        a, b = jnp.split(u, 2, axis=-1)
        x = x + (jax.nn.silu(a) * b) @ w_down[l]
    return x
```

```python
import functools
import math

import jax
import jax.numpy as jnp
from jax import lax
from jax.experimental import pallas as pl
from jax.experimental.pallas import tpu as pltpu

N_HEADS = 8
HEAD_DIM = 64
V_HEAD_DIM = 2 * HEAD_DIM
ROPE_THETA = 10000.0
NORM_EPS = 1e-6
CONV_K = 3
HALO = 8
LANES = 128
VMEM_LIMIT_BYTES = 56 * 1024 * 1024

_F32 = jnp.float32
_BF16 = jnp.bfloat16


def _sigmoid(x):
    return 1.0 / (1.0 + jnp.exp(-x))


def _causal_conv_from_scratch(u, scr, w, first_tile):
    rows = u.shape[0]

    @pl.when(first_tile)
    def _():
        scr[0:HALO, :] = jnp.zeros((HALO, scr.shape[1]), _F32)

    scr[HALO:HALO + rows, :] = u
    y = w[2:3, :] * u + w[1:2, :] * scr[HALO - 1:HALO - 1 + rows, :] + w[0:1, :] * scr[HALO - 2:HALO - 2 + rows, :]
    scr[0:HALO, :] = scr[rows:rows + HALO, :]
    return y


def _proj_kernel(x_ref, g_ref, w_ref, bg_ref, qg_ref, kg_ref, cos_ref, sin_ref, scw_ref, seg_ref,
                 q_ref, k_ref, vt_ref, ga_ref, gc_ref, z_scr, *, tm, bk, d):
    x = x_ref[...]
    ms = jnp.mean(x * x, axis=-1, keepdims=True)
    h = (x * lax.rsqrt(ms + NORM_EPS) * g_ref[...]).astype(_BF16)

    def proj(g):
        return jnp.dot(h, w_ref[:, g * d:(g + 1) * d], preferred_element_type=_F32)

    cosp = cos_ref[...]
    sinp = sin_ref[...]
    seg = seg_ref[...]
    lane = lax.broadcasted_iota(jnp.int32, (tm, LANES), 1)
    first_half = (lane & (HEAD_DIM // 2)) == 0

    def norm_rope(p, gam, o_ref):
        for c in range(d // (2 * LANES)):
            pc = p[:, c * 2 * LANES:(c + 1) * 2 * LANES]
            msq = jnp.dot((pc * pc).astype(_BF16), seg, preferred_element_type=_F32)
            xn = pc * lax.rsqrt(msq + NORM_EPS) * gam[:, c * 2 * LANES:(c + 1) * 2 * LANES]
            for hh in range(2):
                xh = xn[:, hh * LANES:(hh + 1) * LANES]
                rot = jnp.where(first_half, pltpu.roll(xh, LANES - HEAD_DIM // 2, 1), pltpu.roll(xh, HEAD_DIM // 2, 1))
                col = c * 2 * LANES + hh * LANES
                o_ref[:, col:col + LANES] = (xh * cosp + rot * sinp).astype(o_ref.dtype)

    norm_rope(proj(0), qg_ref[...], q_ref)
    norm_rope(proj(1), kg_ref[...], k_ref)

    pv = proj(2)
    for r in range(tm // bk):
        vt_ref[r] = pv[r * bk:(r + 1) * bk, :].T.astype(vt_ref.dtype)

    z = proj(4) * proj(5)
    conv = _causal_conv_from_scratch(z, z_scr, scw_ref[...], pl.program_id(1) == 0)
    bg = bg_ref[...]
    y_conv = proj(3) * conv
    gc_ref[...] = (_sigmoid(proj(7) + bg[:, d:]) * y_conv).astype(gc_ref.dtype)
    ga_ref[...] = _sigmoid(proj(6) + bg[:, :d]).astype(ga_ref.dtype)


def _attn_kernel(lam_ref, q_ref, k_ref, vt_ref, ga_ref, gc_ref, sg_ref, o_ref, acc_ref, *, seq, bq, bk, lambda_init):
    lv = lam_ref[...]
    lam = (jnp.exp(jnp.sum(lv[0:1] * lv[1:2], axis=-1, keepdims=True))
           - jnp.exp(jnp.sum(lv[2:3] * lv[3:4], axis=-1, keepdims=True)) + lambda_init)
    w = 2 * bq
    row = lax.broadcasted_iota(jnp.int32, (V_HEAD_DIM, bq), 0)
    kpos = lax.broadcasted_iota(jnp.int32, (bk, w), 0)
    qpos = lax.broadcasted_iota(jnp.int32, (bk, w), 1) & (bq - 1)
    diag_ok = kpos <= qpos
    sg = sg_ref[...]

    def q_block(qi, carry):
        q0 = pl.multiple_of(qi * bq, bq)
        qt = q_ref[pl.ds(q0, bq), :].astype(_F32).T
        qz = jnp.concatenate([jnp.where(row < HEAD_DIM, qt, 0.0), jnp.where(row >= HEAD_DIM, qt, 0.0)],
                             axis=1).astype(_BF16)
        acc_ref[...] = jnp.zeros_like(acc_ref)

        def kv_step(j, ml, masked):
            m, l = ml
            k0 = pl.multiple_of(j * bk, bk)
            st = jnp.dot(k_ref[pl.ds(k0, bk), :], qz, preferred_element_type=_F32)
            if masked:
                st = jnp.where(diag_ok, st, -jnp.inf)
            m_new = jnp.maximum(m, jnp.max(st, axis=0, keepdims=True))
            alpha = jnp.exp(m - m_new)
            p = jnp.exp(st - m_new)
            l = alpha * l + jnp.sum(p, axis=0, keepdims=True)
            acc_ref[...] = acc_ref[...] * alpha + jnp.dot(vt_ref[j], p.astype(_BF16), preferred_element_type=_F32)
            return m_new, l

        ml = (jnp.full((1, w), -jnp.inf, _F32), jnp.zeros((1, w), _F32))
        ml = lax.fori_loop(0, qi, lambda j, c: kv_step(j, c, False), ml)
        _, l = kv_step(qi, ml, True)

        a = acc_ref[...] * (1.0 / l)
        o = a[:, :bq] - lam * a[:, bq:]
        on = o * lax.rsqrt(jnp.mean(o * o, axis=0, keepdims=True) + NORM_EPS)
        y = on.T * sg * (1.0 - lambda_init)
        ga = ga_ref[pl.ds(q0, bq), :].astype(_F32)
        gc = gc_ref[pl.ds(q0, bq), :].astype(_F32)
        o_ref[pl.ds(q0, bq), :] = (ga * y + gc).astype(o_ref.dtype)
        return carry

    lax.fori_loop(0, seq // bq, q_block, 0)


def _ffn_kernel(x_ref, mg_ref, wo_ref, g_ref, wu_ref, cw_ref, cb_ref, wd_ref, o_ref,
                x1_scr, h_scr, acc_scr, u_scr, carry_scr, *, tm, cf, n_chunks):
    first_tile = pl.program_id(1) == 0
    x1 = x_ref[...] + jnp.dot(mg_ref[...], wo_ref[...], preferred_element_type=_F32)
    x1_scr[...] = x1
    ms = jnp.mean(x1 * x1, axis=-1, keepdims=True)
    h_scr[...] = (x1 * lax.rsqrt(ms + NORM_EPS) * g_ref[...]).astype(_BF16)
    acc_scr[...] = jnp.zeros_like(acc_scr)

    def chunk(c, carry):
        u = jnp.dot(h_scr[...], wu_ref[c], preferred_element_type=_F32)

        @pl.when(first_tile)
        def _():
            carry_scr[c] = jnp.zeros((HALO, 2 * cf), _F32)

        u_scr[0:HALO, :] = carry_scr[c]
        u_scr[HALO:HALO + tm, :] = u
        cw = cw_ref[c]
        uc = (cw[2:3, :] * u + cw[1:2, :] * u_scr[HALO - 1:HALO - 1 + tm, :]
              + cw[0:1, :] * u_scr[HALO - 2:HALO - 2 + tm, :] + cb_ref[c])
        carry_scr[c] = u_scr[tm:tm + HALO, :]
        a = uc[:, :cf]
        act = (a * _sigmoid(a) * uc[:, cf:]).astype(_BF16)
        acc_scr[...] += jnp.dot(act, wd_ref[c], preferred_element_type=_F32)
        return carry

    lax.fori_loop(0, n_chunks, chunk, 0)
    o_ref[...] = x1_scr[...] + acc_scr[...]


def _const_spec(shape):
    nd = len(shape)
    return pl.BlockSpec(shape, lambda *_: (0,) * nd, pipeline_mode=pl.Buffered(1))


def _rope_tables(seq):
    inv = ROPE_THETA ** (-jnp.arange(0, HEAD_DIM, 2, dtype=_F32) / HEAD_DIM)
    ang = jnp.arange(seq, dtype=_F32)[:, None] * inv[None, :]
    c, s = jnp.cos(ang), jnp.sin(ang)
    return jnp.concatenate([c, c, c, c], axis=1), jnp.concatenate([-s, s, -s, s], axis=1)


def kernel(x, attn_norm_g, w_in, b_gate, q_norm_g, k_norm_g, lambda_q1, lambda_k1, lambda_q2, lambda_k2,
           subln_g, short_conv_w, w_out, ffn_norm_g, w_up, ffn_conv_w, ffn_conv_b, w_down):
    bsz, seq, d = x.shape
    depth = attn_norm_g.shape[0]
    d_ff = w_down.shape[1]
    assert d == N_HEADS * V_HEAD_DIM and w_in.shape[2] == 8 * d

    tm1 = min(512, seq)
    bq = bk = min(256, seq)
    tm3 = min(256, seq)
    cf = 256
    n_chunks = d_ff // cf
    assert seq % tm1 == 0 and seq % bq == 0 and seq % tm3 == 0 and d_ff % cf == 0 and tm1 % bk == 0

    cosp, sinp = _rope_tables(seq)
    seg = jnp.kron(jnp.eye(2 * LANES // HEAD_DIM, dtype=_F32), jnp.full((HEAD_DIM, HEAD_DIM), 1.0 / HEAD_DIM, _F32)).astype(_BF16)
    params = pltpu.CompilerParams(dimension_semantics=("arbitrary", "arbitrary"), vmem_limit_bytes=VMEM_LIMIT_BYTES)

    for l in range(depth):
        lambda_init = 0.8 - 0.6 * math.exp(-0.3 * l)
        qg = jnp.tile(q_norm_g[l] * HEAD_DIM ** -0.5, 2 * N_HEADS)[None, :]
        kg = jnp.tile(k_norm_g[l], 2 * N_HEADS)[None, :]

        act = jax.ShapeDtypeStruct((bsz, seq, d), _BF16)
        tok1 = pl.BlockSpec((None, tm1, d), lambda b, i: (b, i, 0))
        q, k, vt, ga, gc = pl.pallas_call(
            functools.partial(_proj_kernel, tm=tm1, bk=bk, d=d),
            grid=(bsz, seq // tm1),
            in_specs=[tok1, _const_spec((1, d)), _const_spec((d, 8 * d)), _const_spec((1, 2 * d)),
                      _const_spec((1, d)), _const_spec((1, d)),
                      pl.BlockSpec((tm1, LANES), lambda b, i: (i, 0)), pl.BlockSpec((tm1, LANES), lambda b, i: (i, 0)),
                      _const_spec((CONV_K, d)), _const_spec((2 * LANES, 2 * LANES))],
            out_specs=[tok1, tok1, pl.BlockSpec((None, tm1 // bk, d, bk), lambda b, i: (b, i, 0, 0)), tok1, tok1],
            out_shape=[act, act, jax.ShapeDtypeStruct((bsz, seq // bk, d, bk), _BF16), act, act],
            scratch_shapes=[pltpu.VMEM((tm1 + HALO, d), _F32)],
            compiler_params=params, name="norm_proj",
        )(x, attn_norm_g[l][None, :], w_in[l].astype(_BF16), b_gate[l][None, :], qg, kg, cosp, sinp,
          short_conv_w[l], seg)

        lamv = jnp.stack([lambda_q1[l], lambda_k1[l], lambda_q2[l], lambda_k2[l]]).astype(_F32)
        head = pl.BlockSpec((None, seq, V_HEAD_DIM), lambda b, hd: (b, 0, hd))
        merged = pl.pallas_call(
            functools.partial(_attn_kernel, seq=seq, bq=bq, bk=bk, lambda_init=lambda_init),
            grid=(bsz, N_HEADS),
            in_specs=[_const_spec((4, HEAD_DIM)), head, head,
                      pl.BlockSpec((None, seq // bk, V_HEAD_DIM, bk), lambda b, hd: (b, 0, hd, 0)),
                      head, head, _const_spec((1, V_HEAD_DIM))],
            out_specs=head,
            out_shape=act,
            scratch_shapes=[pltpu.VMEM((V_HEAD_DIM, 2 * bq), _F32)],
            compiler_params=params, name="diff_attn",
        )(lamv, q, k, vt, ga, gc, subln_g[l][None, :])

        wu = w_up[l].astype(_BF16).reshape(d, 2, n_chunks, cf).transpose(2, 0, 1, 3).reshape(n_chunks, d, 2 * cf)
        cw = ffn_conv_w[l].reshape(CONV_K, 2, n_chunks, cf).transpose(2, 0, 1, 3).reshape(n_chunks, CONV_K, 2 * cf)
        cb = ffn_conv_b[l].reshape(2, n_chunks, cf).transpose(1, 0, 2).reshape(n_chunks, 1, 2 * cf)
        wd = w_down[l].astype(_BF16).reshape(n_chunks, cf, d)
        tok3 = pl.BlockSpec((None, tm3, d), lambda b, i: (b, i, 0))
        x = pl.pallas_call(
            functools.partial(_ffn_kernel, tm=tm3, cf=cf, n_chunks=n_chunks),
            grid=(bsz, seq // tm3),
            in_specs=[tok3, tok3, _const_spec((d, d)), _const_spec((1, d)), _const_spec((n_chunks, d, 2 * cf)),
                      _const_spec((n_chunks, CONV_K, 2 * cf)), _const_spec((n_chunks, 1, 2 * cf)),
                      _const_spec((n_chunks, cf, d))],
            out_specs=tok3,
            out_shape=jax.ShapeDtypeStruct((bsz, seq, d), x.dtype),
            scratch_shapes=[pltpu.VMEM((tm3, d), _F32), pltpu.VMEM((tm3, d), _BF16), pltpu.VMEM((tm3, d), _F32),
                            pltpu.VMEM((tm3 + HALO, 2 * cf), _F32), pltpu.VMEM((n_chunks, HALO, 2 * cf), _F32)],
            compiler_params=params, name="out_proj_ffn",
        )(x, merged, w_out[l].astype(_BF16), ffn_norm_g[l][None, :], wu, cw, cb, wd)
    return x
```

```python
import functools
import math

import jax
import jax.numpy as jnp
from jax import lax
from jax.experimental import pallas as pl
from jax.experimental.pallas import tpu as pltpu

N_HEADS = 8
HEAD_DIM = 64
V_HEAD_DIM = 2 * HEAD_DIM
ROPE_THETA = 10000.0
NORM_EPS = 1e-6
CONV_K = 3
HALO = 8
LANES = 128
VMEM_LIMIT_BYTES = 56 * 1024 * 1024

_F32 = jnp.float32
_BF16 = jnp.bfloat16


def _sigmoid(x):
    return 1.0 / (1.0 + jnp.exp(-x))


def _causal_conv_from_scratch(u, scr, w, first_tile):
    rows = u.shape[0]

    @pl.when(first_tile)
    def _():
        scr[0:HALO, :] = jnp.zeros((HALO, scr.shape[1]), _F32)

    scr[HALO:HALO + rows, :] = u
    y = w[2:3, :] * u + w[1:2, :] * scr[HALO - 1:HALO - 1 + rows, :] + w[0:1, :] * scr[HALO - 2:HALO - 2 + rows, :]
    scr[0:HALO, :] = scr[rows:rows + HALO, :]
    return y


def _proj_kernel(x_ref, g_ref, w_ref, bg_ref, qg_ref, kg_ref, cos_ref, sin_ref, scw_ref, seg_ref,
                 q_ref, k_ref, vt_ref, ga_ref, gc_ref, z_scr, *, tm, bk, d):
    x = x_ref[...]
    ms = jnp.mean(x * x, axis=-1, keepdims=True)
    h = (x * lax.rsqrt(ms + NORM_EPS) * g_ref[...]).astype(_BF16)

    def proj(g):
        return jnp.dot(h, w_ref[:, g * d:(g + 1) * d], preferred_element_type=_F32)

    cosp = cos_ref[...]
    sinp = sin_ref[...]
    seg = seg_ref[...]
    lane = lax.broadcasted_iota(jnp.int32, (tm, LANES), 1)
    first_half = (lane & (HEAD_DIM // 2)) == 0

    def norm_rope(p, gam, o_ref):
        for c in range(d // (2 * LANES)):
            pc = p[:, c * 2 * LANES:(c + 1) * 2 * LANES]
            msq = jnp.dot((pc * pc).astype(_BF16), seg, preferred_element_type=_F32)
            xn = pc * lax.rsqrt(msq + NORM_EPS) * gam[:, c * 2 * LANES:(c + 1) * 2 * LANES]
            for hh in range(2):
                xh = xn[:, hh * LANES:(hh + 1) * LANES]
                rot = jnp.where(first_half, pltpu.roll(xh, LANES - HEAD_DIM // 2, 1), pltpu.roll(xh, HEAD_DIM // 2, 1))
                col = c * 2 * LANES + hh * LANES
                o_ref[:, col:col + LANES] = (xh * cosp + rot * sinp).astype(o_ref.dtype)

    norm_rope(proj(0), qg_ref[...], q_ref)
    norm_rope(proj(1), kg_ref[...], k_ref)

    pv = proj(2)
    for r in range(tm // bk):
        vt_ref[r] = pv[r * bk:(r + 1) * bk, :].T.astype(vt_ref.dtype)

    z = proj(4) * proj(5)
    conv = _causal_conv_from_scratch(z, z_scr, scw_ref[...], pl.program_id(1) == 0)
    bg = bg_ref[...]
    y_conv = proj(3) * conv
    gc_ref[...] = (_sigmoid(proj(7) + bg[:, d:]) * y_conv).astype(gc_ref.dtype)
    ga_ref[...] = _sigmoid(proj(6) + bg[:, :d]).astype(ga_ref.dtype)


def _attn_kernel(lam_ref, q_ref, k_ref, vt_ref, ga_ref, gc_ref, sg_ref, o_ref,
                 qz_scr, st0, st1, acc_ref, m_ref, l_ref, *, seq, bk, lambda_init):
    bq = 2 * bk
    lv = lam_ref[...]
    lam = (jnp.exp(jnp.sum(lv[0:1] * lv[1:2], axis=-1, keepdims=True))
           - jnp.exp(jnp.sum(lv[2:3] * lv[3:4], axis=-1, keepdims=True)) + lambda_init)
    row = lax.broadcasted_iota(jnp.int32, (V_HEAD_DIM, bk), 0)
    diag_ok = lax.broadcasted_iota(jnp.int32, (bk, bk), 0) <= lax.broadcasted_iota(jnp.int32, (bk, bk), 1)
    sg = sg_ref[...]
    all_groups = (0, 1, 2, 3)

    def scores(j, st, groups):
        kb = k_ref[pl.ds(pl.multiple_of(j * bk, bk), bk), :]
        for g in groups:
            st[:, g * bk:(g + 1) * bk] = jnp.dot(kb, qz_scr[g], preferred_element_type=_F32)

    def absorb(j, st, groups, masked):
        vt = vt_ref[j]
        for g in groups:
            cols = slice(g * bk, (g + 1) * bk)
            s = st[:, cols]
            if g in masked:
                s = jnp.where(diag_ok, s, -jnp.inf)
            m_old = m_ref[:, cols]
            m_new = jnp.maximum(m_old, jnp.max(s, axis=0, keepdims=True))
            alpha = jnp.exp2(m_old - m_new)
            p = jnp.exp2(s - m_new)
            l_ref[:, cols] = alpha * l_ref[:, cols] + jnp.sum(p, axis=0, keepdims=True)
            m_ref[:, cols] = m_new
            acc_ref[:, cols] = acc_ref[:, cols] * alpha + jnp.dot(vt, p.astype(_BF16), preferred_element_type=_F32)

    def q_block(qi, carry):
        q0 = pl.multiple_of(qi * bq, bq)
        qt = q_ref[pl.ds(q0, bq), :].astype(_F32).T
        for g in all_groups:
            qh = qt[:, (g % 2) * bk:(g % 2 + 1) * bk]
            keep = (row < HEAD_DIM) if g // 2 == 0 else (row >= HEAD_DIM)
            qz_scr[g] = jnp.where(keep, qh, 0.0).astype(_BF16)
        acc_ref[...] = jnp.zeros_like(acc_ref)
        m_ref[...] = jnp.full_like(m_ref, -jnp.inf)
        l_ref[...] = jnp.zeros_like(l_ref)

        scores(0, st0, all_groups)

        def pair(t, c):
            j = 2 * t
            scores(j + 1, st1, all_groups)
            absorb(j, st0, all_groups, ())
            scores(j + 2, st0, all_groups)
            absorb(j + 1, st1, all_groups, ())
            return c

        lax.fori_loop(0, qi, pair, 0)
        jd = 2 * qi
        scores(jd + 1, st1, (1, 3))
        absorb(jd, st0, all_groups, (0, 2))
        absorb(jd + 1, st1, (1, 3), (1, 3))

        a = acc_ref[...] * (1.0 / l_ref[...])
        o = a[:, :bq] - lam * a[:, bq:]
        on = o * lax.rsqrt(jnp.mean(o * o, axis=0, keepdims=True) + NORM_EPS)
        y = on.T * sg * (1.0 - lambda_init)
        ga = ga_ref[pl.ds(q0, bq), :].astype(_F32)
        gc = gc_ref[pl.ds(q0, bq), :].astype(_F32)
        o_ref[pl.ds(q0, bq), :] = (ga * y + gc).astype(o_ref.dtype)
        return carry

    lax.fori_loop(0, seq // bq, q_block, 0)


def _ffn_kernel(x_ref, mg_ref, wo_ref, g_ref, wu_ref, cw_ref, cb_ref, wd_ref, o_ref,
                h_scr, u_scr, *, tm, cf, n_chunks):
    @pl.when(pl.program_id(1) == 0)
    def _():
        for c in range(n_chunks):
            u_scr[c, 0:HALO, :] = jnp.zeros((HALO, 2 * cf), _F32)

    x1 = x_ref[...] + jnp.dot(mg_ref[...], wo_ref[...], preferred_element_type=_F32)
    ms = jnp.mean(x1 * x1, axis=-1, keepdims=True)
    h_scr[...] = (x1 * lax.rsqrt(ms + NORM_EPS) * g_ref[...]).astype(_BF16)

    def up(c):
        return jnp.dot(h_scr[...], wu_ref[c], preferred_element_type=_F32)

    acc = x1
    u_next = up(0)
    for c in range(n_chunks):
        u = u_next
        if c + 1 < n_chunks:
            u_next = up(c + 1)
        u_scr[c, HALO:HALO + tm, :] = u
        cw = cw_ref[c]
        uc = (cw[2:3, :] * u + cw[1:2, :] * u_scr[c, HALO - 1:HALO - 1 + tm, :]
              + cw[0:1, :] * u_scr[c, HALO - 2:HALO - 2 + tm, :] + cb_ref[c])
        u_scr[c, 0:HALO, :] = u_scr[c, tm:tm + HALO, :]
        a = uc[:, :cf]
        act = (a * _sigmoid(a) * uc[:, cf:]).astype(_BF16)
        acc = acc + jnp.dot(act, wd_ref[c], preferred_element_type=_F32)
    o_ref[...] = acc


def _const_spec(shape):
    nd = len(shape)
    return pl.BlockSpec(shape, lambda *_: (0,) * nd, pipeline_mode=pl.Buffered(1))


def _rope_tables(seq):
    inv = ROPE_THETA ** (-jnp.arange(0, HEAD_DIM, 2, dtype=_F32) / HEAD_DIM)
    ang = jnp.arange(seq, dtype=_F32)[:, None] * inv[None, :]
    c, s = jnp.cos(ang), jnp.sin(ang)
    return jnp.concatenate([c, c, c, c], axis=1), jnp.concatenate([-s, s, -s, s], axis=1)


def kernel(x, attn_norm_g, w_in, b_gate, q_norm_g, k_norm_g, lambda_q1, lambda_k1, lambda_q2, lambda_k2,
           subln_g, short_conv_w, w_out, ffn_norm_g, w_up, ffn_conv_w, ffn_conv_b, w_down):
    bsz, seq, d = x.shape
    depth = attn_norm_g.shape[0]
    d_ff = w_down.shape[1]
    assert d == N_HEADS * V_HEAD_DIM and w_in.shape[2] == 8 * d

    tm1 = min(512, seq)
    bk = 256
    bq = 2 * bk
    tm3 = min(256, seq)
    cf = 256
    n_chunks = d_ff // cf
    assert seq % tm1 == 0 and seq % bq == 0 and seq % tm3 == 0 and d_ff % cf == 0 and tm1 % bk == 0

    cosp, sinp = _rope_tables(seq)
    seg = jnp.kron(jnp.eye(2 * LANES // HEAD_DIM, dtype=_F32), jnp.full((HEAD_DIM, HEAD_DIM), 1.0 / HEAD_DIM, _F32)).astype(_BF16)
    params = pltpu.CompilerParams(dimension_semantics=("arbitrary", "arbitrary"), vmem_limit_bytes=VMEM_LIMIT_BYTES)

    for l in range(depth):
        lambda_init = 0.8 - 0.6 * math.exp(-0.3 * l)
        qg = jnp.tile(q_norm_g[l] * (HEAD_DIM ** -0.5 * math.log2(math.e)), 2 * N_HEADS)[None, :]
        kg = jnp.tile(k_norm_g[l], 2 * N_HEADS)[None, :]

        act = jax.ShapeDtypeStruct((bsz, seq, d), _BF16)
        tok1 = pl.BlockSpec((None, tm1, d), lambda b, i: (b, i, 0))
        q, k, vt, ga, gc = pl.pallas_call(
            functools.partial(_proj_kernel, tm=tm1, bk=bk, d=d),
            grid=(bsz, seq // tm1),
            in_specs=[tok1, _const_spec((1, d)), _const_spec((d, 8 * d)), _const_spec((1, 2 * d)),
                      _const_spec((1, d)), _const_spec((1, d)),
                      pl.BlockSpec((tm1, LANES), lambda b, i: (i, 0)), pl.BlockSpec((tm1, LANES), lambda b, i: (i, 0)),
                      _const_spec((CONV_K, d)), _const_spec((2 * LANES, 2 * LANES))],
            out_specs=[tok1, tok1, pl.BlockSpec((None, tm1 // bk, d, bk), lambda b, i: (b, i, 0, 0)), tok1, tok1],
            out_shape=[act, act, jax.ShapeDtypeStruct((bsz, seq // bk, d, bk), _BF16), act, act],
            scratch_shapes=[pltpu.VMEM((tm1 + HALO, d), _F32)],
            compiler_params=params, name="norm_proj",
        )(x, attn_norm_g[l][None, :], w_in[l].astype(_BF16), b_gate[l][None, :], qg, kg, cosp, sinp,
          short_conv_w[l], seg)

        lamv = jnp.stack([lambda_q1[l], lambda_k1[l], lambda_q2[l], lambda_k2[l]]).astype(_F32)
        head = pl.BlockSpec((None, seq, V_HEAD_DIM), lambda b, hd: (b, 0, hd))
        merged = pl.pallas_call(
            functools.partial(_attn_kernel, seq=seq, bk=bk, lambda_init=lambda_init),
            grid=(bsz, N_HEADS),
            in_specs=[_const_spec((4, HEAD_DIM)), head, head,
                      pl.BlockSpec((None, seq // bk, V_HEAD_DIM, bk), lambda b, hd: (b, 0, hd, 0)),
                      head, head, _const_spec((1, V_HEAD_DIM))],
            out_specs=head,
            out_shape=act,
            scratch_shapes=[pltpu.VMEM((4, V_HEAD_DIM, bk), _BF16), pltpu.VMEM((bk, 2 * bq), _F32),
                            pltpu.VMEM((bk, 2 * bq), _F32), pltpu.VMEM((V_HEAD_DIM, 2 * bq), _F32),
                            pltpu.VMEM((1, 2 * bq), _F32), pltpu.VMEM((1, 2 * bq), _F32)],
            compiler_params=params, name="diff_attn",
        )(lamv, q, k, vt, ga, gc, subln_g[l][None, :])

        wu = w_up[l].astype(_BF16).reshape(d, 2, n_chunks, cf).transpose(2, 0, 1, 3).reshape(n_chunks, d, 2 * cf)
        cw = ffn_conv_w[l].reshape(CONV_K, 2, n_chunks, cf).transpose(2, 0, 1, 3).reshape(n_chunks, CONV_K, 2 * cf)
        cb = ffn_conv_b[l].reshape(2, n_chunks, cf).transpose(1, 0, 2).reshape(n_chunks, 1, 2 * cf)
        wd = w_down[l].astype(_BF16).reshape(n_chunks, cf, d)
        tok3 = pl.BlockSpec((None, tm3, d), lambda b, i: (b, i, 0))
        x = pl.pallas_call(
            functools.partial(_ffn_kernel, tm=tm3, cf=cf, n_chunks=n_chunks),
            grid=(bsz, seq // tm3),
            in_specs=[tok3, tok3, _const_spec((d, d)), _const_spec((1, d)), _const_spec((n_chunks, d, 2 * cf)),
                      _const_spec((n_chunks, CONV_K, 2 * cf)), _const_spec((n_chunks, 1, 2 * cf)),
                      _const_spec((n_chunks, cf, d))],
            out_specs=tok3,
            out_shape=jax.ShapeDtypeStruct((bsz, seq, d), x.dtype),
            scratch_shapes=[pltpu.VMEM((tm3, d), _BF16), pltpu.VMEM((n_chunks, tm3 + HALO, 2 * cf), _F32)],
            compiler_params=params, name="out_proj_ffn",
        )(x, merged, w_out[l].astype(_BF16), ffn_norm_g[l][None, :], wu, cw, cb, wd)
    return x
```

```python
import functools
import math

import jax
import jax.numpy as jnp
from jax import lax
from jax.experimental import pallas as pl
from jax.experimental.pallas import tpu as pltpu

N_HEADS = 8
HEAD_DIM = 64
V_HEAD_DIM = 2 * HEAD_DIM
ROPE_THETA = 10000.0
NORM_EPS = 1e-6
CONV_K = 3
HALO = 8
LANES = 128
SUM_ROWS = 16
VMEM_LIMIT_BYTES = 56 * 1024 * 1024

_F32 = jnp.float32
_BF16 = jnp.bfloat16


def _sigmoid(x):
    return 1.0 / (1.0 + jnp.exp(-x))


def _causal_conv_from_scratch(u, scr, w, first_tile):
    rows = u.shape[0]

    @pl.when(first_tile)
    def _():
        scr[0:HALO, :] = jnp.zeros((HALO, scr.shape[1]), _F32)

    scr[HALO:HALO + rows, :] = u
    y = w[2:3, :] * u + w[1:2, :] * scr[HALO - 1:HALO - 1 + rows, :] + w[0:1, :] * scr[HALO - 2:HALO - 2 + rows, :]
    scr[0:HALO, :] = scr[rows:rows + HALO, :]
    return y


def _proj_kernel(x_ref, g_ref, w_ref, bg_ref, qg_ref, kg_ref, cos_ref, sin_ref, scw_ref, seg_ref,
                 q_ref, k_ref, vt_ref, ga_ref, gc_ref, z_scr, *, tm, bk, d):
    x = x_ref[...]
    ms = jnp.mean(x * x, axis=-1, keepdims=True)
    h = (x * lax.rsqrt(ms + NORM_EPS) * g_ref[...]).astype(_BF16)

    def proj(g):
        return jnp.dot(h, w_ref[:, g * d:(g + 1) * d], preferred_element_type=_F32)

    cosp = cos_ref[...]
    sinp = sin_ref[...]
    seg = seg_ref[...]
    lane = lax.broadcasted_iota(jnp.int32, (tm, LANES), 1)
    first_half = (lane & (HEAD_DIM // 2)) == 0

    def norm_rope(p, gam, o_ref):
        for c in range(d // (2 * LANES)):
            pc = p[:, c * 2 * LANES:(c + 1) * 2 * LANES]
            msq = jnp.dot((pc * pc).astype(_BF16), seg, preferred_element_type=_F32)
            xn = pc * lax.rsqrt(msq + NORM_EPS) * gam[:, c * 2 * LANES:(c + 1) * 2 * LANES]
            for hh in range(2):
                xh = xn[:, hh * LANES:(hh + 1) * LANES]
                rot = jnp.where(first_half, pltpu.roll(xh, LANES - HEAD_DIM // 2, 1), pltpu.roll(xh, HEAD_DIM // 2, 1))
                col = c * 2 * LANES + hh * LANES
                o_ref[:, col:col + LANES] = (xh * cosp + rot * sinp).astype(o_ref.dtype)

    norm_rope(proj(0), qg_ref[...], q_ref)
    norm_rope(proj(1), kg_ref[...], k_ref)

    pv = proj(2)
    for r in range(tm // bk):
        vt_ref[r] = pv[r * bk:(r + 1) * bk, :].T.astype(vt_ref.dtype)

    z = proj(4) * proj(5)
    conv = _causal_conv_from_scratch(z, z_scr, scw_ref[...], pl.program_id(1) == 0)
    bg = bg_ref[...]
    y_conv = proj(3) * conv
    gc_ref[...] = (_sigmoid(proj(7) + bg[:, d:]) * y_conv).astype(gc_ref.dtype)
    ga_ref[...] = _sigmoid(proj(6) + bg[:, :d]).astype(ga_ref.dtype)


def _attn_kernel(lam_ref, q_ref, k_ref, vt_ref, ga_ref, gc_ref, sg_ref, o_ref,
                 qz_scr, st_a, st_b, acc_ref, m_ref, *, seq, bk, nq, lambda_init):
    bq = nq * bk
    lv = lam_ref[...]
    lam = (jnp.exp(jnp.sum(lv[0:1] * lv[1:2], axis=-1, keepdims=True))
           - jnp.exp(jnp.sum(lv[2:3] * lv[3:4], axis=-1, keepdims=True)) + lambda_init)
    row = lax.broadcasted_iota(jnp.int32, (V_HEAD_DIM, bk), 0)
    diag_ok = lax.broadcasted_iota(jnp.int32, (bk, bk), 0) <= lax.broadcasted_iota(jnp.int32, (bk, bk), 1)
    sg = sg_ref[...]
    ones_rows = jnp.ones((SUM_ROWS, bk), _BF16)
    all_groups = tuple(range(2 * nq))

    def parts_from(c0):
        return tuple(g for g in all_groups if g % nq >= c0)

    def scores(j, st, groups):
        kb = k_ref[pl.ds(pl.multiple_of(j * bk, bk), bk), :]
        for g in groups:
            st[g] = jnp.dot(kb, qz_scr[g], preferred_element_type=_F32)

    def absorb(j, st, groups, masked_part):
        vt = jnp.concatenate([vt_ref[j], ones_rows], axis=0)
        for g in groups:
            s = st[g]
            if g % nq == masked_part:
                s = jnp.where(diag_ok, s, -jnp.inf)
            m_old = m_ref[g]
            m_new = jnp.maximum(m_old, jnp.max(s, axis=0, keepdims=True))
            alpha = jnp.exp2(m_old - m_new)
            p = jnp.exp2(s - m_new).astype(_BF16)
            m_ref[g] = m_new
            acc_ref[g] = acc_ref[g] * alpha + jnp.dot(vt, p, preferred_element_type=_F32)

    def q_block(qi, carry):
        q0 = pl.multiple_of(qi * bq, bq)
        qt = q_ref[pl.ds(q0, bq), :].astype(_F32).T
        for g in all_groups:
            qh = qt[:, (g % nq) * bk:(g % nq + 1) * bk]
            keep = (row < HEAD_DIM) if g // nq == 0 else (row >= HEAD_DIM)
            qz_scr[g] = jnp.where(keep, qh, 0.0).astype(_BF16)
        acc_ref[...] = jnp.zeros_like(acc_ref)
        m_ref[...] = jnp.full_like(m_ref, -jnp.inf)

        scores(0, st_a, all_groups)

        def pair(t, c):
            j = 2 * t
            scores(j + 1, st_b, all_groups)
            absorb(j, st_a, all_groups, None)
            scores(j + 2, st_a, all_groups)
            absorb(j + 1, st_b, all_groups, None)
            return c

        lax.fori_loop(0, (nq // 2) * qi, pair, 0)
        jd = nq * qi
        for r in range(nq):
            cur, nxt = (st_a, st_b) if r % 2 == 0 else (st_b, st_a)
            if r + 1 < nq:
                scores(jd + r + 1, nxt, parts_from(r + 1))
            absorb(jd + r, cur, parts_from(r), r)

        def normalized(g):
            acc = acc_ref[g]
            return acc[:V_HEAD_DIM] * (1.0 / acc[V_HEAD_DIM:V_HEAD_DIM + 1])

        for c in range(nq):
            o = normalized(c) - lam * normalized(nq + c)
            on = o * lax.rsqrt(jnp.mean(o * o, axis=0, keepdims=True) + NORM_EPS)
            y = on.T * sg * (1.0 - lambda_init)
            rows = pl.ds(q0 + c * bk, bk)
            o_ref[rows, :] = (ga_ref[rows, :].astype(_F32) * y + gc_ref[rows, :].astype(_F32)).astype(o_ref.dtype)
        return carry

    lax.fori_loop(0, seq // bq, q_block, 0)


def _ffn_kernel(x_ref, mg_ref, wo_ref, g_ref, wu_ref, cw_ref, cb_ref, wd_ref, o_ref,
                h_scr, u_scr, *, tm, cf, n_chunks):
    @pl.when(pl.program_id(1) == 0)
    def _():
        for c in range(n_chunks):
            u_scr[c, 0:HALO, :] = jnp.zeros((HALO, 2 * cf), _F32)

    x1 = x_ref[...] + jnp.dot(mg_ref[...], wo_ref[...], preferred_element_type=_F32)
    ms = jnp.mean(x1 * x1, axis=-1, keepdims=True)
    h_scr[...] = (x1 * lax.rsqrt(ms + NORM_EPS) * g_ref[...]).astype(_BF16)

    def up(c):
        return jnp.dot(h_scr[...], wu_ref[c], preferred_element_type=_F32)

    acc = x1
    u_next = up(0)
    for c in range(n_chunks):
        u = u_next
        if c + 1 < n_chunks:
            u_next = up(c + 1)
        u_scr[c, HALO:HALO + tm, :] = u
        cw = cw_ref[c]
        uc = (cw[2:3, :] * u + cw[1:2, :] * u_scr[c, HALO - 1:HALO - 1 + tm, :]
              + cw[0:1, :] * u_scr[c, HALO - 2:HALO - 2 + tm, :] + cb_ref[c])
        u_scr[c, 0:HALO, :] = u_scr[c, tm:tm + HALO, :]
        a = uc[:, :cf]
        act = (a * _sigmoid(a) * uc[:, cf:]).astype(_BF16)
        acc = acc + jnp.dot(act, wd_ref[c], preferred_element_type=_F32)
    o_ref[...] = acc


def _const_spec(shape):
    nd = len(shape)
    return pl.BlockSpec(shape, lambda *_: (0,) * nd, pipeline_mode=pl.Buffered(1))


def _rope_tables(seq):
    inv = ROPE_THETA ** (-jnp.arange(0, HEAD_DIM, 2, dtype=_F32) / HEAD_DIM)
    ang = jnp.arange(seq, dtype=_F32)[:, None] * inv[None, :]
    c, s = jnp.cos(ang), jnp.sin(ang)
    return jnp.concatenate([c, c, c, c], axis=1), jnp.concatenate([-s, s, -s, s], axis=1)


def kernel(x, attn_norm_g, w_in, b_gate, q_norm_g, k_norm_g, lambda_q1, lambda_k1, lambda_q2, lambda_k2,
           subln_g, short_conv_w, w_out, ffn_norm_g, w_up, ffn_conv_w, ffn_conv_b, w_down):
    bsz, seq, d = x.shape
    depth = attn_norm_g.shape[0]
    d_ff = w_down.shape[1]
    assert d == N_HEADS * V_HEAD_DIM and w_in.shape[2] == 8 * d

    tm1 = min(512, seq)
    bk = 256
    nq = 4 if seq % (4 * bk) == 0 else 2
    bq = nq * bk
    tm3 = min(256, seq)
    cf = 256
    n_chunks = d_ff // cf
    assert seq % tm1 == 0 and seq % bq == 0 and seq % tm3 == 0 and d_ff % cf == 0 and tm1 % bk == 0

    cosp, sinp = _rope_tables(seq)
    seg = jnp.kron(jnp.eye(2 * LANES // HEAD_DIM, dtype=_F32), jnp.full((HEAD_DIM, HEAD_DIM), 1.0 / HEAD_DIM, _F32)).astype(_BF16)
    params = pltpu.CompilerParams(dimension_semantics=("arbitrary", "arbitrary"), vmem_limit_bytes=VMEM_LIMIT_BYTES)

    for l in range(depth):
        lambda_init = 0.8 - 0.6 * math.exp(-0.3 * l)
        qg = jnp.tile(q_norm_g[l] * (HEAD_DIM ** -0.5 * math.log2(math.e)), 2 * N_HEADS)[None, :]
        kg = jnp.tile(k_norm_g[l], 2 * N_HEADS)[None, :]

        act = jax.ShapeDtypeStruct((bsz, seq, d), _BF16)
        tok1 = pl.BlockSpec((None, tm1, d), lambda b, i: (b, i, 0))
        q, k, vt, ga, gc = pl.pallas_call(
            functools.partial(_proj_kernel, tm=tm1, bk=bk, d=d),
            grid=(bsz, seq // tm1),
            in_specs=[tok1, _const_spec((1, d)), _const_spec((d, 8 * d)), _const_spec((1, 2 * d)),
                      _const_spec((1, d)), _const_spec((1, d)),
                      pl.BlockSpec((tm1, LANES), lambda b, i: (i, 0)), pl.BlockSpec((tm1, LANES), lambda b, i: (i, 0)),
                      _const_spec((CONV_K, d)), _const_spec((2 * LANES, 2 * LANES))],
            out_specs=[tok1, tok1, pl.BlockSpec((None, tm1 // bk, d, bk), lambda b, i: (b, i, 0, 0)), tok1, tok1],
            out_shape=[act, act, jax.ShapeDtypeStruct((bsz, seq // bk, d, bk), _BF16), act, act],
            scratch_shapes=[pltpu.VMEM((tm1 + HALO, d), _F32)],
            compiler_params=params, name="norm_proj",
        )(x, attn_norm_g[l][None, :], w_in[l].astype(_BF16), b_gate[l][None, :], qg, kg, cosp, sinp,
          short_conv_w[l], seg)

        lamv = jnp.stack([lambda_q1[l], lambda_k1[l], lambda_q2[l], lambda_k2[l]]).astype(_F32)
        head = pl.BlockSpec((None, seq, V_HEAD_DIM), lambda b, hd: (b, 0, hd))
        merged = pl.pallas_call(
            functools.partial(_attn_kernel, seq=seq, bk=bk, nq=nq, lambda_init=lambda_init),
            grid=(bsz, N_HEADS),
            in_specs=[_const_spec((4, HEAD_DIM)), head, head,
                      pl.BlockSpec((None, seq // bk, V_HEAD_DIM, bk), lambda b, hd: (b, 0, hd, 0)),
                      head, head, _const_spec((1, V_HEAD_DIM))],
            out_specs=head,
            out_shape=act,
            scratch_shapes=[pltpu.VMEM((2 * nq, V_HEAD_DIM, bk), _BF16),
                            pltpu.VMEM((2 * nq, bk, bk), _F32), pltpu.VMEM((2 * nq, bk, bk), _F32),
                            pltpu.VMEM((2 * nq, V_HEAD_DIM + SUM_ROWS, bk), _F32), pltpu.VMEM((2 * nq, 1, bk), _F32)],
            compiler_params=params, name="diff_attn",
        )(lamv, q, k, vt, ga, gc, subln_g[l][None, :])

        wu = w_up[l].astype(_BF16).reshape(d, 2, n_chunks, cf).transpose(2, 0, 1, 3).reshape(n_chunks, d, 2 * cf)
        cw = ffn_conv_w[l].reshape(CONV_K, 2, n_chunks, cf).transpose(2, 0, 1, 3).reshape(n_chunks, CONV_K, 2 * cf)
        cb = ffn_conv_b[l].reshape(2, n_chunks, cf).transpose(1, 0, 2).reshape(n_chunks, 1, 2 * cf)
        wd = w_down[l].astype(_BF16).reshape(n_chunks, cf, d)
        tok3 = pl.BlockSpec((None, tm3, d), lambda b, i: (b, i, 0))
        x = pl.pallas_call(
            functools.partial(_ffn_kernel, tm=tm3, cf=cf, n_chunks=n_chunks),
            grid=(bsz, seq // tm3),
            in_specs=[tok3, tok3, _const_spec((d, d)), _const_spec((1, d)), _const_spec((n_chunks, d, 2 * cf)),
                      _const_spec((n_chunks, CONV_K, 2 * cf)), _const_spec((n_chunks, 1, 2 * cf)),
                      _const_spec((n_chunks, cf, d))],
            out_specs=tok3,
            out_shape=jax.ShapeDtypeStruct((bsz, seq, d), x.dtype),
            scratch_shapes=[pltpu.VMEM((tm3, d), _BF16), pltpu.VMEM((n_chunks, tm3 + HALO, 2 * cf), _F32)],
            compiler_params=params, name="out_proj_ffn",
        )(x, merged, w_out[l].astype(_BF16), ffn_norm_g[l][None, :], wu, cw, cb, wd)
    return x
```

```python
import functools
import math

import jax
import jax.numpy as jnp
from jax import lax
from jax.experimental import pallas as pl
from jax.experimental.pallas import tpu as pltpu

N_HEADS = 8
HEAD_DIM = 64
V_HEAD_DIM = 2 * HEAD_DIM
ROPE_THETA = 10000.0
NORM_EPS = 1e-6
CONV_K = 3
HALO = 8
LANES = 128
SUM_ROWS = 16
VMEM_LIMIT_BYTES = 56 * 1024 * 1024

_F32 = jnp.float32
_BF16 = jnp.bfloat16


def _sigmoid(x):
    return 1.0 / (1.0 + jnp.exp(-x))


def _causal_conv_from_scratch(u, scr, w, first_tile):
    rows = u.shape[0]

    @pl.when(first_tile)
    def _():
        scr[0:HALO, :] = jnp.zeros((HALO, scr.shape[1]), _F32)

    scr[HALO:HALO + rows, :] = u
    y = w[2:3, :] * u + w[1:2, :] * scr[HALO - 1:HALO - 1 + rows, :] + w[0:1, :] * scr[HALO - 2:HALO - 2 + rows, :]
    scr[0:HALO, :] = scr[rows:rows + HALO, :]
    return y


def _proj_kernel(x_ref, g_ref, w_ref, bg_ref, qg_ref, kg_ref, cos_ref, sin_ref, scw_ref, seg_ref,
                 q_ref, k_ref, vt_ref, ga_ref, gc_ref, z_scr, *, tm, bk, d):
    x = x_ref[...]
    ms = jnp.mean(x * x, axis=-1, keepdims=True)
    h = (x * lax.rsqrt(ms + NORM_EPS) * g_ref[...]).astype(_BF16)

    def proj(g):
        return jnp.dot(h, w_ref[:, g * d:(g + 1) * d], preferred_element_type=_F32)

    cosp = cos_ref[...]
    sinp = sin_ref[...]
    seg = seg_ref[...]
    lane = lax.broadcasted_iota(jnp.int32, (tm, LANES), 1)
    first_half = (lane & (HEAD_DIM // 2)) == 0

    def norm_rope(p, gam, o_ref):
        for c in range(d // (2 * LANES)):
            pc = p[:, c * 2 * LANES:(c + 1) * 2 * LANES]
            msq = jnp.dot((pc * pc).astype(_BF16), seg, preferred_element_type=_F32)
            xn = pc * lax.rsqrt(msq + NORM_EPS) * gam[:, c * 2 * LANES:(c + 1) * 2 * LANES]
            for hh in range(2):
                xh = xn[:, hh * LANES:(hh + 1) * LANES]
                rot = jnp.where(first_half, pltpu.roll(xh, LANES - HEAD_DIM // 2, 1), pltpu.roll(xh, HEAD_DIM // 2, 1))
                col = c * 2 * LANES + hh * LANES
                o_ref[:, col:col + LANES] = (xh * cosp + rot * sinp).astype(o_ref.dtype)

    norm_rope(proj(0), qg_ref[...], q_ref)
    norm_rope(proj(1), kg_ref[...], k_ref)

    pv = proj(2)
    for r in range(tm // bk):
        vt_ref[r] = pv[r * bk:(r + 1) * bk, :].T.astype(vt_ref.dtype)

    z = proj(4) * proj(5)
    conv = _causal_conv_from_scratch(z, z_scr, scw_ref[...], pl.program_id(1) == 0)
    bg = bg_ref[...]
    y_conv = proj(3) * conv
    gc_ref[...] = (_sigmoid(proj(7) + bg[:, d:]) * y_conv).astype(gc_ref.dtype)
    ga_ref[...] = _sigmoid(proj(6) + bg[:, :d]).astype(ga_ref.dtype)


def _attn_kernel(lam_ref, q_ref, k_ref, vt_ref, ga_ref, gc_ref, sg_ref, o_ref,
                 qz_scr, st_a, st_b, acc_ref, m_ref, *, seq, bk, nq, lambda_init):
    bq = nq * bk
    lv = lam_ref[...]
    lam = (jnp.exp(jnp.sum(lv[0:1] * lv[1:2], axis=-1, keepdims=True))
           - jnp.exp(jnp.sum(lv[2:3] * lv[3:4], axis=-1, keepdims=True)) + lambda_init)
    row = lax.broadcasted_iota(jnp.int32, (V_HEAD_DIM, bk), 0)
    diag_ok = lax.broadcasted_iota(jnp.int32, (bk, bk), 0) <= lax.broadcasted_iota(jnp.int32, (bk, bk), 1)
    sg = sg_ref[...]
    ones_rows = jnp.ones((SUM_ROWS, bk), _BF16)
    all_groups = tuple(range(2 * nq))

    def parts_from(c0):
        return tuple(g for g in all_groups if g % nq >= c0)

    def scores(j, st, groups):
        kb = k_ref[pl.ds(pl.multiple_of(j * bk, bk), bk), :]
        for g in groups:
            st[g] = jnp.dot(kb, qz_scr[g], preferred_element_type=_F32)

    def absorb(j, st, groups, masked_part):
        vt = jnp.concatenate([vt_ref[j], ones_rows], axis=0)
        for g in groups:
            s = st[g]
            if g % nq == masked_part:
                s = jnp.where(diag_ok, s, -jnp.inf)
            m_old = m_ref[g]
            m_new = jnp.maximum(m_old, jnp.max(s, axis=0, keepdims=True))
            alpha = jnp.exp2(m_old - m_new)
            p = jnp.exp2(s - m_new).astype(_BF16)
            m_ref[g] = m_new
            acc_ref[g] = acc_ref[g] * alpha + jnp.dot(vt, p, preferred_element_type=_F32)

    def load_queries(qi):
        q0 = pl.multiple_of(qi * bq, bq)
        qt = q_ref[pl.ds(q0, bq), :].astype(_F32).T
        for g in all_groups:
            qh = qt[:, (g % nq) * bk:(g % nq + 1) * bk]
            keep = (row < HEAD_DIM) if g // nq == 0 else (row >= HEAD_DIM)
            qz_scr[g] = jnp.where(keep, qh, 0.0).astype(_BF16)
        scores(0, st_a, all_groups)

    def reset_state():
        acc_ref[...] = jnp.zeros_like(acc_ref)
        m_ref[...] = jnp.full_like(m_ref, -jnp.inf)

    def visible_blocks(t, c):
        for r in range(nq):
            cur, nxt = (st_a, st_b) if r % 2 == 0 else (st_b, st_a)
            for g in all_groups:
                scores(nq * t + r + 1, nxt, (g,))
                absorb(nq * t + r, cur, (g,), None)
        return c

    def normalized(g):
        acc = acc_ref[g]
        return acc[:V_HEAD_DIM] * (1.0 / acc[V_HEAD_DIM:V_HEAD_DIM + 1])

    n_qblocks = seq // bq

    def q_block(qi, carry):
        lax.fori_loop(0, qi, visible_blocks, 0)
        jd = nq * qi
        for r in range(nq):
            cur, nxt = (st_a, st_b) if r % 2 == 0 else (st_b, st_a)
            for g in parts_from(r):
                if g % nq > r:
                    scores(jd + r + 1, nxt, (g,))
                absorb(jd + r, cur, (g,), r)

        load_queries(jnp.minimum(qi + 1, n_qblocks - 1))

        q0 = pl.multiple_of(qi * bq, bq)
        for c in range(nq):
            o = normalized(c) - lam * normalized(nq + c)
            on = o * lax.rsqrt(jnp.mean(o * o, axis=0, keepdims=True) + NORM_EPS)
            y = on.T * sg * (1.0 - lambda_init)
            rows = pl.ds(q0 + c * bk, bk)
            o_ref[rows, :] = (ga_ref[rows, :].astype(_F32) * y + gc_ref[rows, :].astype(_F32)).astype(o_ref.dtype)
        reset_state()
        return carry

    load_queries(0)
    reset_state()
    lax.fori_loop(0, n_qblocks, q_block, 0)


def _ffn_kernel(x_ref, mg_ref, wo_ref, g_ref, wu_ref, cw_ref, cb_ref, wd_ref, o_ref,
                h_scr, u_scr, *, tm, cf, n_chunks):
    @pl.when(pl.program_id(1) == 0)
    def _():
        for c in range(n_chunks):
            u_scr[c, 0:HALO, :] = jnp.zeros((HALO, 2 * cf), _F32)

    x1 = x_ref[...] + jnp.dot(mg_ref[...], wo_ref[...], preferred_element_type=_F32)
    ms = jnp.mean(x1 * x1, axis=-1, keepdims=True)
    h_scr[...] = (x1 * lax.rsqrt(ms + NORM_EPS) * g_ref[...]).astype(_BF16)

    def up(c):
        return jnp.dot(h_scr[...], wu_ref[c], preferred_element_type=_F32)

    acc = x1
    u_next = up(0)
    for c in range(n_chunks):
        u = u_next
        if c + 1 < n_chunks:
            u_next = up(c + 1)
        u_scr[c, HALO:HALO + tm, :] = u
        cw = cw_ref[c]
        uc = (cw[2:3, :] * u + cw[1:2, :] * u_scr[c, HALO - 1:HALO - 1 + tm, :]
              + cw[0:1, :] * u_scr[c, HALO - 2:HALO - 2 + tm, :] + cb_ref[c])
        u_scr[c, 0:HALO, :] = u_scr[c, tm:tm + HALO, :]
        a = uc[:, :cf]
        act = (a * _sigmoid(a) * uc[:, cf:]).astype(_BF16)
        acc = acc + jnp.dot(act, wd_ref[c], preferred_element_type=_F32)
    o_ref[...] = acc


def _const_spec(shape):
    nd = len(shape)
    return pl.BlockSpec(shape, lambda *_: (0,) * nd, pipeline_mode=pl.Buffered(1))


def _rope_tables(seq):
    inv = ROPE_THETA ** (-jnp.arange(0, HEAD_DIM, 2, dtype=_F32) / HEAD_DIM)
    ang = jnp.arange(seq, dtype=_F32)[:, None] * inv[None, :]
    c, s = jnp.cos(ang), jnp.sin(ang)
    return jnp.concatenate([c, c, c, c], axis=1), jnp.concatenate([-s, s, -s, s], axis=1)


def kernel(x, attn_norm_g, w_in, b_gate, q_norm_g, k_norm_g, lambda_q1, lambda_k1, lambda_q2, lambda_k2,
           subln_g, short_conv_w, w_out, ffn_norm_g, w_up, ffn_conv_w, ffn_conv_b, w_down):
    bsz, seq, d = x.shape
    depth = attn_norm_g.shape[0]
    d_ff = w_down.shape[1]
    assert d == N_HEADS * V_HEAD_DIM and w_in.shape[2] == 8 * d

    tm1 = min(512, seq)
    bk = 256
    nq = 4 if seq % (4 * bk) == 0 else 2
    bq = nq * bk
    tm3 = min(256, seq)
    cf = 256
    n_chunks = d_ff // cf
    assert seq % tm1 == 0 and seq % bq == 0 and seq % tm3 == 0 and d_ff % cf == 0 and tm1 % bk == 0

    cosp, sinp = _rope_tables(seq)
    seg = jnp.kron(jnp.eye(2 * LANES // HEAD_DIM, dtype=_F32), jnp.full((HEAD_DIM, HEAD_DIM), 1.0 / HEAD_DIM, _F32)).astype(_BF16)
    params = pltpu.CompilerParams(dimension_semantics=("arbitrary", "arbitrary"), vmem_limit_bytes=VMEM_LIMIT_BYTES)

    for l in range(depth):
        lambda_init = 0.8 - 0.6 * math.exp(-0.3 * l)
        qg = jnp.tile(q_norm_g[l] * (HEAD_DIM ** -0.5 * math.log2(math.e)), 2 * N_HEADS)[None, :]
        kg = jnp.tile(k_norm_g[l], 2 * N_HEADS)[None, :]

        act = jax.ShapeDtypeStruct((bsz, seq, d), _BF16)
        tok1 = pl.BlockSpec((None, tm1, d), lambda b, i: (b, i, 0))
        q, k, vt, ga, gc = pl.pallas_call(
            functools.partial(_proj_kernel, tm=tm1, bk=bk, d=d),
            grid=(bsz, seq // tm1),
            in_specs=[tok1, _const_spec((1, d)), _const_spec((d, 8 * d)), _const_spec((1, 2 * d)),
                      _const_spec((1, d)), _const_spec((1, d)),
                      pl.BlockSpec((tm1, LANES), lambda b, i: (i, 0)), pl.BlockSpec((tm1, LANES), lambda b, i: (i, 0)),
                      _const_spec((CONV_K, d)), _const_spec((2 * LANES, 2 * LANES))],
            out_specs=[tok1, tok1, pl.BlockSpec((None, tm1 // bk, d, bk), lambda b, i: (b, i, 0, 0)), tok1, tok1],
            out_shape=[act, act, jax.ShapeDtypeStruct((bsz, seq // bk, d, bk), _BF16), act, act],
            scratch_shapes=[pltpu.VMEM((tm1 + HALO, d), _F32)],
            compiler_params=params, name="norm_proj",
        )(x, attn_norm_g[l][None, :], w_in[l].astype(_BF16), b_gate[l][None, :], qg, kg, cosp, sinp,
          short_conv_w[l], seg)

        lamv = jnp.stack([lambda_q1[l], lambda_k1[l], lambda_q2[l], lambda_k2[l]]).astype(_F32)
        head = pl.BlockSpec((None, seq, V_HEAD_DIM), lambda b, hd: (b, 0, hd))
        merged = pl.pallas_call(
            functools.partial(_attn_kernel, seq=seq, bk=bk, nq=nq, lambda_init=lambda_init),
            grid=(bsz, N_HEADS),
            in_specs=[_const_spec((4, HEAD_DIM)), head, head,
                      pl.BlockSpec((None, seq // bk, V_HEAD_DIM, bk), lambda b, hd: (b, 0, hd, 0)),
                      head, head, _const_spec((1, V_HEAD_DIM))],
            out_specs=head,
            out_shape=act,
            scratch_shapes=[pltpu.VMEM((2 * nq, V_HEAD_DIM, bk), _BF16),
                            pltpu.VMEM((2 * nq, bk, bk), _F32), pltpu.VMEM((2 * nq, bk, bk), _F32),
                            pltpu.VMEM((2 * nq, V_HEAD_DIM + SUM_ROWS, bk), _F32), pltpu.VMEM((2 * nq, 1, bk), _F32)],
            compiler_params=params, name="diff_attn",
        )(lamv, q, k, vt, ga, gc, subln_g[l][None, :])

        wu = w_up[l].astype(_BF16).reshape(d, 2, n_chunks, cf).transpose(2, 0, 1, 3).reshape(n_chunks, d, 2 * cf)
        cw = ffn_conv_w[l].reshape(CONV_K, 2, n_chunks, cf).transpose(2, 0, 1, 3).reshape(n_chunks, CONV_K, 2 * cf)
        cb = ffn_conv_b[l].reshape(2, n_chunks, cf).transpose(1, 0, 2).reshape(n_chunks, 1, 2 * cf)
        wd = w_down[l].astype(_BF16).reshape(n_chunks, cf, d)
        tok3 = pl.BlockSpec((None, tm3, d), lambda b, i: (b, i, 0))
        x = pl.pallas_call(
            functools.partial(_ffn_kernel, tm=tm3, cf=cf, n_chunks=n_chunks),
            grid=(bsz, seq // tm3),
            in_specs=[tok3, tok3, _const_spec((d, d)), _const_spec((1, d)), _const_spec((n_chunks, d, 2 * cf)),
                      _const_spec((n_chunks, CONV_K, 2 * cf)), _const_spec((n_chunks, 1, 2 * cf)),
                      _const_spec((n_chunks, cf, d))],
            out_specs=tok3,
            out_shape=jax.ShapeDtypeStruct((bsz, seq, d), x.dtype),
            scratch_shapes=[pltpu.VMEM((tm3, d), _BF16), pltpu.VMEM((n_chunks, tm3 + HALO, 2 * cf), _F32)],
            compiler_params=params, name="out_proj_ffn",
        )(x, merged, w_out[l].astype(_BF16), ffn_norm_g[l][None, :], wu, cw, cb, wd)
    return x
```

```python
import functools
import math

import jax
import jax.numpy as jnp
from jax import lax
from jax.experimental import pallas as pl
from jax.experimental.pallas import tpu as pltpu

N_HEADS = 8
HEAD_DIM = 64
V_HEAD_DIM = 2 * HEAD_DIM
ROPE_THETA = 10000.0
NORM_EPS = 1e-6
CONV_K = 3
HALO = 8
LANES = 128
SUM_ROWS = 16
LEAD = 1
VMEM_LIMIT_BYTES = 56 * 1024 * 1024

_F32 = jnp.float32
_BF16 = jnp.bfloat16


def _sigmoid(x):
    return 1.0 / (1.0 + jnp.exp(-x))


def _proj_kernel(x_ref, g_ref, w_ref, bg_ref, qg_ref, kg_ref, cos_ref, sin_ref, scw_ref, seg_ref,
                 q_ref, k_ref, vt_ref, ga_ref, gc_ref, z_scr, *, tm, bk, d):
    x = x_ref[...]
    ms = jnp.mean(x * x, axis=-1, keepdims=True)
    h = (x * lax.rsqrt(ms + NORM_EPS) * g_ref[...]).astype(_BF16)

    cw = 2 * LANES
    n_pieces = d // cw

    def proj(g, c):
        col = g * d + c * cw
        return jnp.dot(h, w_ref[:, col:col + cw], preferred_element_type=_F32)

    cosp = cos_ref[...]
    sinp = sin_ref[...]
    seg = seg_ref[...]
    lane = lax.broadcasted_iota(jnp.int32, (tm, LANES), 1)
    first_half = (lane & (HEAD_DIM // 2)) == 0
    first_tile = pl.program_id(1) == 0

    @pl.when(first_tile)
    def _():
        z_scr[0:HALO, :] = jnp.zeros((HALO, d), _F32)

    def norm_rope(pc, gam_ref, o_ref, c):
        msq = jnp.dot((pc * pc).astype(_BF16), seg, preferred_element_type=_F32)
        xn = pc * lax.rsqrt(msq + NORM_EPS) * gam_ref[:, c * cw:(c + 1) * cw]
        for hh in range(cw // LANES):
            xh = xn[:, hh * LANES:(hh + 1) * LANES]
            rot = jnp.where(first_half, pltpu.roll(xh, LANES - HEAD_DIM // 2, 1), pltpu.roll(xh, HEAD_DIM // 2, 1))
            col = c * cw + hh * LANES
            o_ref[:, col:col + LANES] = (xh * cosp + rot * sinp).astype(o_ref.dtype)

    def q_piece(ps, c):
        norm_rope(ps[0], qg_ref, q_ref, c)

    def k_piece(ps, c):
        norm_rope(ps[0], kg_ref, k_ref, c)

    def v_piece(ps, c):
        for r in range(tm // bk):
            vt_ref[r, c * cw:(c + 1) * cw, :] = ps[0][r * bk:(r + 1) * bk, :].T.astype(vt_ref.dtype)

    def conv_piece(ps, c):
        p_b, p_c, p_x, p_g = ps
        cols = slice(c * cw, (c + 1) * cw)
        z = p_c * p_x
        w = scw_ref[:, cols]
        z_scr[HALO:HALO + tm, cols] = z
        conv = (w[2:3, :] * z + w[1:2, :] * z_scr[HALO - 1:HALO - 1 + tm, cols]
                + w[0:1, :] * z_scr[HALO - 2:HALO - 2 + tm, cols])
        z_scr[0:HALO, cols] = z_scr[tm:tm + HALO, cols]
        gate = _sigmoid(p_g + bg_ref[:, d + c * cw:d + (c + 1) * cw])
        gc_ref[:, cols] = (gate * (p_b * conv)).astype(gc_ref.dtype)

    def ga_piece(ps, c):
        cols = slice(c * cw, (c + 1) * cw)
        ga_ref[:, cols] = _sigmoid(ps[0] + bg_ref[:, cols]).astype(ga_ref.dtype)

    work = [((0,), q_piece), ((1,), k_piece), ((2,), v_piece), ((3, 4, 5, 7), conv_piece), ((6,), ga_piece)]
    units = [(groups, fn, c) for groups, fn in work for c in range(n_pieces)]
    pending = [proj(g, units[0][2]) for g in units[0][0]]
    for i, (groups, fn, c) in enumerate(units):
        ps = pending
        if i + 1 < len(units):
            pending = [proj(g, units[i + 1][2]) for g in units[i + 1][0]]
        fn(ps, c)


def _attn_kernel(lam_ref, q_ref, k_ref, vt_ref, ga_ref, gc_ref, sg_ref, o_ref,
                 qz_scr, st_a, st_b, acc_ref, m_ref, *, seq, bk, nq, lambda_init):
    bq = nq * bk
    lv = lam_ref[...]
    lam = (jnp.exp(jnp.sum(lv[0:1] * lv[1:2], axis=-1, keepdims=True))
           - jnp.exp(jnp.sum(lv[2:3] * lv[3:4], axis=-1, keepdims=True)) + lambda_init)
    row = lax.broadcasted_iota(jnp.int32, (V_HEAD_DIM, bk), 0)
    diag_ok = lax.broadcasted_iota(jnp.int32, (bk, bk), 0) <= lax.broadcasted_iota(jnp.int32, (bk, bk), 1)
    sg = sg_ref[...]
    ones_rows = jnp.ones((SUM_ROWS, bk), _BF16)
    all_groups = tuple(range(2 * nq))

    def parts_from(c0):
        return tuple(g for g in all_groups if g % nq >= c0)

    def scores(j, st, groups):
        kb = k_ref[pl.ds(pl.multiple_of(j * bk, bk), bk), :]
        for g in groups:
            st[g] = jnp.dot(kb, qz_scr[g], preferred_element_type=_F32)

    def absorb(j, st, groups, masked_part):
        vt = jnp.concatenate([vt_ref[j], ones_rows], axis=0)
        for g in groups:
            s = st[g]
            if g % nq == masked_part:
                s = jnp.where(diag_ok, s, -jnp.inf)
            m_old = m_ref[g]
            m_new = jnp.maximum(m_old, jnp.max(s, axis=0, keepdims=True))
            alpha = jnp.exp2(m_old - m_new)
            p = jnp.exp2(s - m_new).astype(_BF16)
            m_ref[g] = m_new
            acc_ref[g] = acc_ref[g] * alpha + jnp.dot(vt, p, preferred_element_type=_F32)

    def load_queries(qi):
        q0 = pl.multiple_of(qi * bq, bq)
        qt = q_ref[pl.ds(q0, bq), :].astype(_F32).T
        for g in all_groups:
            qh = qt[:, (g % nq) * bk:(g % nq + 1) * bk]
            keep = (row < HEAD_DIM) if g // nq == 0 else (row >= HEAD_DIM)
            qz_scr[g] = jnp.where(keep, qh, 0.0).astype(_BF16)
        scores(0, st_a, all_groups)

    def reset_state():
        acc_ref[...] = jnp.zeros_like(acc_ref)
        m_ref[...] = jnp.full_like(m_ref, -jnp.inf)

    def visible_blocks(t, c):
        for r in range(nq):
            cur, nxt = (st_a, st_b) if r % 2 == 0 else (st_b, st_a)
            scores(nq * t + r + 1, nxt, all_groups[:LEAD])
            for i, g in enumerate(all_groups):
                scores(nq * t + r + 1, nxt, all_groups[i + LEAD:i + LEAD + 1])
                absorb(nq * t + r, cur, (g,), None)
        return c

    def normalized(g):
        acc = acc_ref[g]
        return acc[:V_HEAD_DIM] * (1.0 / acc[V_HEAD_DIM:V_HEAD_DIM + 1])

    n_qblocks = seq // bq

    def q_block(qi, carry):
        lax.fori_loop(0, qi, visible_blocks, 0)
        jd = nq * qi
        for r in range(nq):
            cur, nxt = (st_a, st_b) if r % 2 == 0 else (st_b, st_a)
            ahead = parts_from(r + 1)
            scores(jd + r + 1, nxt, ahead[:LEAD])
            for i, g in enumerate(parts_from(r)):
                scores(jd + r + 1, nxt, ahead[i + LEAD:i + LEAD + 1])
                absorb(jd + r, cur, (g,), r)

        load_queries(jnp.minimum(qi + 1, n_qblocks - 1))

        q0 = pl.multiple_of(qi * bq, bq)
        for c in range(nq):
            o = normalized(c) - lam * normalized(nq + c)
            on = o * lax.rsqrt(jnp.mean(o * o, axis=0, keepdims=True) + NORM_EPS)
            y = on.T * sg * (1.0 - lambda_init)
            rows = pl.ds(q0 + c * bk, bk)
            o_ref[rows, :] = (ga_ref[rows, :].astype(_F32) * y + gc_ref[rows, :].astype(_F32)).astype(o_ref.dtype)
        reset_state()
        return carry

    load_queries(0)
    reset_state()
    lax.fori_loop(0, n_qblocks, q_block, 0)


def _ffn_kernel(x_ref, mg_ref, wo_ref, g_ref, wu_ref, cw_ref, cb_ref, wd_ref, o_ref,
                h_scr, u_scr, *, tm, cf, n_chunks):
    @pl.when(pl.program_id(1) == 0)
    def _():
        for c in range(n_chunks):
            u_scr[c, 0:HALO, :] = jnp.zeros((HALO, 2 * cf), _F32)

    x1 = x_ref[...] + jnp.dot(mg_ref[...], wo_ref[...], preferred_element_type=_F32)
    ms = jnp.mean(x1 * x1, axis=-1, keepdims=True)
    h_scr[...] = (x1 * lax.rsqrt(ms + NORM_EPS) * g_ref[...]).astype(_BF16)

    def up(c):
        return jnp.dot(h_scr[...], wu_ref[c], preferred_element_type=_F32)

    acc = x1
    u_next = up(0)
    for c in range(n_chunks):
        u = u_next
        if c + 1 < n_chunks:
            u_next = up(c + 1)
        u_scr[c, HALO:HALO + tm, :] = u
        cw = cw_ref[c]
        uc = (cw[2:3, :] * u + cw[1:2, :] * u_scr[c, HALO - 1:HALO - 1 + tm, :]
              + cw[0:1, :] * u_scr[c, HALO - 2:HALO - 2 + tm, :] + cb_ref[c])
        u_scr[c, 0:HALO, :] = u_scr[c, tm:tm + HALO, :]
        a = uc[:, :cf]
        act = (a * _sigmoid(a) * uc[:, cf:]).astype(_BF16)
        acc = acc + jnp.dot(act, wd_ref[c], preferred_element_type=_F32)
    o_ref[...] = acc


def _const_spec(shape):
    nd = len(shape)
    return pl.BlockSpec(shape, lambda *_: (0,) * nd, pipeline_mode=pl.Buffered(1))


def _rope_tables(seq):
    inv = ROPE_THETA ** (-jnp.arange(0, HEAD_DIM, 2, dtype=_F32) / HEAD_DIM)
    ang = jnp.arange(seq, dtype=_F32)[:, None] * inv[None, :]
    c, s = jnp.cos(ang), jnp.sin(ang)
    return jnp.concatenate([c, c, c, c], axis=1), jnp.concatenate([-s, s, -s, s], axis=1)


def kernel(x, attn_norm_g, w_in, b_gate, q_norm_g, k_norm_g, lambda_q1, lambda_k1, lambda_q2, lambda_k2,
           subln_g, short_conv_w, w_out, ffn_norm_g, w_up, ffn_conv_w, ffn_conv_b, w_down):
    bsz, seq, d = x.shape
    depth = attn_norm_g.shape[0]
    d_ff = w_down.shape[1]
    assert d == N_HEADS * V_HEAD_DIM and w_in.shape[2] == 8 * d

    tm1 = min(512, seq)
    bk = 256
    nq = 4 if seq % (4 * bk) == 0 else 2
    bq = nq * bk
    tm3 = min(256, seq)
    cf = 256
    n_chunks = d_ff // cf
    assert seq % tm1 == 0 and seq % bq == 0 and seq % tm3 == 0 and d_ff % cf == 0 and tm1 % bk == 0

    cosp, sinp = _rope_tables(seq)
    seg = jnp.kron(jnp.eye(2 * LANES // HEAD_DIM, dtype=_F32), jnp.full((HEAD_DIM, HEAD_DIM), 1.0 / HEAD_DIM, _F32)).astype(_BF16)
    params = pltpu.CompilerParams(dimension_semantics=("arbitrary", "arbitrary"), vmem_limit_bytes=VMEM_LIMIT_BYTES)

    for l in range(depth):
        lambda_init = 0.8 - 0.6 * math.exp(-0.3 * l)
        qg = jnp.tile(q_norm_g[l] * (HEAD_DIM ** -0.5 * math.log2(math.e)), 2 * N_HEADS)[None, :]
        kg = jnp.tile(k_norm_g[l], 2 * N_HEADS)[None, :]

        act = jax.ShapeDtypeStruct((bsz, seq, d), _BF16)
        tok1 = pl.BlockSpec((None, tm1, d), lambda b, i: (b, i, 0))
        q, k, vt, ga, gc = pl.pallas_call(
            functools.partial(_proj_kernel, tm=tm1, bk=bk, d=d),
            grid=(bsz, seq // tm1),
            in_specs=[tok1, _const_spec((1, d)), _const_spec((d, 8 * d)), _const_spec((1, 2 * d)),
                      _const_spec((1, d)), _const_spec((1, d)),
                      pl.BlockSpec((tm1, LANES), lambda b, i: (i, 0)), pl.BlockSpec((tm1, LANES), lambda b, i: (i, 0)),
                      _const_spec((CONV_K, d)), _const_spec((2 * LANES, 2 * LANES))],
            out_specs=[tok1, tok1, pl.BlockSpec((None, tm1 // bk, d, bk), lambda b, i: (b, i, 0, 0)), tok1, tok1],
            out_shape=[act, act, jax.ShapeDtypeStruct((bsz, seq // bk, d, bk), _BF16), act, act],
            scratch_shapes=[pltpu.VMEM((tm1 + HALO, d), _F32)],
            compiler_params=params, name="norm_proj",
        )(x, attn_norm_g[l][None, :], w_in[l].astype(_BF16), b_gate[l][None, :], qg, kg, cosp, sinp,
          short_conv_w[l], seg)

        lamv = jnp.stack([lambda_q1[l], lambda_k1[l], lambda_q2[l], lambda_k2[l]]).astype(_F32)
        head = pl.BlockSpec((None, seq, V_HEAD_DIM), lambda b, hd: (b, 0, hd))
        merged = pl.pallas_call(
            functools.partial(_attn_kernel, seq=seq, bk=bk, nq=nq, lambda_init=lambda_init),
            grid=(bsz, N_HEADS),
            in_specs=[_const_spec((4, HEAD_DIM)), head, head,
                      pl.BlockSpec((None, seq // bk, V_HEAD_DIM, bk), lambda b, hd: (b, 0, hd, 0)),
                      head, head, _const_spec((1, V_HEAD_DIM))],
            out_specs=head,
            out_shape=act,
            scratch_shapes=[pltpu.VMEM((2 * nq, V_HEAD_DIM, bk), _BF16),
                            pltpu.VMEM((2 * nq, bk, bk), _F32), pltpu.VMEM((2 * nq, bk, bk), _F32),
                            pltpu.VMEM((2 * nq, V_HEAD_DIM + SUM_ROWS, bk), _F32), pltpu.VMEM((2 * nq, 1, bk), _F32)],
            compiler_params=params, name="diff_attn",
        )(lamv, q, k, vt, ga, gc, subln_g[l][None, :])

        wu = w_up[l].astype(_BF16).reshape(d, 2, n_chunks, cf).transpose(2, 0, 1, 3).reshape(n_chunks, d, 2 * cf)
        cw = ffn_conv_w[l].reshape(CONV_K, 2, n_chunks, cf).transpose(2, 0, 1, 3).reshape(n_chunks, CONV_K, 2 * cf)
        cb = ffn_conv_b[l].reshape(2, n_chunks, cf).transpose(1, 0, 2).reshape(n_chunks, 1, 2 * cf)
        wd = w_down[l].astype(_BF16).reshape(n_chunks, cf, d)
        tok3 = pl.BlockSpec((None, tm3, d), lambda b, i: (b, i, 0))
        x = pl.pallas_call(
            functools.partial(_ffn_kernel, tm=tm3, cf=cf, n_chunks=n_chunks),
            grid=(bsz, seq // tm3),
            in_specs=[tok3, tok3, _const_spec((d, d)), _const_spec((1, d)), _const_spec((n_chunks, d, 2 * cf)),
                      _const_spec((n_chunks, CONV_K, 2 * cf)), _const_spec((n_chunks, 1, 2 * cf)),
                      _const_spec((n_chunks, cf, d))],
            out_specs=tok3,
            out_shape=jax.ShapeDtypeStruct((bsz, seq, d), x.dtype),
            scratch_shapes=[pltpu.VMEM((tm3, d), _BF16), pltpu.VMEM((n_chunks, tm3 + HALO, 2 * cf), _F32)],
            compiler_params=params, name="out_proj_ffn",
        )(x, merged, w_out[l].astype(_BF16), ffn_norm_g[l][None, :], wu, cw, cb, wd)
    return x
```

```python
import functools
import math

import jax
import jax.numpy as jnp
from jax import lax
from jax.experimental import pallas as pl
from jax.experimental.pallas import tpu as pltpu

N_HEADS = 8
HEAD_DIM = 64
V_HEAD_DIM = 2 * HEAD_DIM
ROPE_THETA = 10000.0
NORM_EPS = 1e-6
CONV_K = 3
HALO = 8
LANES = 128
SUM_ROWS = 16
LEAD = 1
VMEM_LIMIT_BYTES = 56 * 1024 * 1024

_F32 = jnp.float32
_BF16 = jnp.bfloat16


def _sigmoid(x):
    return 1.0 / (1.0 + jnp.exp(-x))


def _proj_kernel(x_ref, g_ref, w_ref, bg_ref, qg_ref, kg_ref, cos_ref, sin_ref, scw_ref, seg_ref,
                 q_ref, k_ref, vt_ref, ga_ref, gc_ref, z_scr, *, tm, bk, d):
    x = x_ref[...]
    ms = jnp.mean(x * x, axis=-1, keepdims=True)
    h = (x * lax.rsqrt(ms + NORM_EPS) * g_ref[...]).astype(_BF16)

    cw = 2 * LANES
    n_pieces = d // cw

    def proj(g, c):
        col = g * d + c * cw
        return jnp.dot(h, w_ref[:, col:col + cw], preferred_element_type=_F32)

    cosp = cos_ref[...]
    sinp = sin_ref[...]
    seg = seg_ref[...]
    lane = lax.broadcasted_iota(jnp.int32, (tm, LANES), 1)
    first_half = (lane & (HEAD_DIM // 2)) == 0
    first_tile = pl.program_id(1) == 0

    @pl.when(first_tile)
    def _():
        z_scr[0:HALO, :] = jnp.zeros((HALO, d), _F32)

    def norm_rope(pc, gam_ref, o_ref, c):
        msq = jnp.dot((pc * pc).astype(_BF16), seg, preferred_element_type=_F32)
        xn = pc * lax.rsqrt(msq + NORM_EPS) * gam_ref[:, c * cw:(c + 1) * cw]
        for hh in range(cw // LANES):
            xh = xn[:, hh * LANES:(hh + 1) * LANES]
            rot = jnp.where(first_half, pltpu.roll(xh, LANES - HEAD_DIM // 2, 1), pltpu.roll(xh, HEAD_DIM // 2, 1))
            col = c * cw + hh * LANES
            o_ref[:, col:col + LANES] = (xh * cosp + rot * sinp).astype(o_ref.dtype)

    def q_piece(ps, c):
        norm_rope(ps[0], qg_ref, q_ref, c)

    def k_piece(ps, c):
        norm_rope(ps[0], kg_ref, k_ref, c)

    def v_piece(ps, c):
        for r in range(tm // bk):
            vt_ref[r, c * cw:(c + 1) * cw, :] = ps[0][r * bk:(r + 1) * bk, :].T.astype(vt_ref.dtype)

    def conv_piece(ps, c):
        p_b, p_c, p_x, p_g = ps
        cols = slice(c * cw, (c + 1) * cw)
        z = p_c * p_x
        w = scw_ref[:, cols]
        z_scr[HALO:HALO + tm, cols] = z
        conv = (w[2:3, :] * z + w[1:2, :] * z_scr[HALO - 1:HALO - 1 + tm, cols]
                + w[0:1, :] * z_scr[HALO - 2:HALO - 2 + tm, cols])
        z_scr[0:HALO, cols] = z_scr[tm:tm + HALO, cols]
        gate = _sigmoid(p_g + bg_ref[:, d + c * cw:d + (c + 1) * cw])
        gc_ref[:, cols] = (gate * (p_b * conv)).astype(gc_ref.dtype)

    def ga_piece(ps, c):
        cols = slice(c * cw, (c + 1) * cw)
        ga_ref[:, cols] = _sigmoid(ps[0] + bg_ref[:, cols]).astype(ga_ref.dtype)

    work = [((0,), q_piece), ((1,), k_piece), ((2,), v_piece), ((3, 4, 5, 7), conv_piece), ((6,), ga_piece)]
    units = [(groups, fn, c) for groups, fn in work for c in range(n_pieces)]
    pending = [proj(g, units[0][2]) for g in units[0][0]]
    for i, (groups, fn, c) in enumerate(units):
        ps = pending
        if i + 1 < len(units):
            pending = [proj(g, units[i + 1][2]) for g in units[i + 1][0]]
        fn(ps, c)


def _attn_kernel(lam_ref, q_ref, k_ref, vt_ref, ga_ref, gc_ref, sg_ref, o_ref,
                 qz_scr, st_a, st_b, mx_a, mx_b, acc_ref, m_ref, *, seq, bk, nq, lambda_init):
    bq = nq * bk
    lv = lam_ref[...]
    lam = (jnp.exp(jnp.sum(lv[0:1] * lv[1:2], axis=-1, keepdims=True))
           - jnp.exp(jnp.sum(lv[2:3] * lv[3:4], axis=-1, keepdims=True)) + lambda_init)
    row = lax.broadcasted_iota(jnp.int32, (V_HEAD_DIM, bk), 0)
    diag_ok = lax.broadcasted_iota(jnp.int32, (bk, bk), 0) <= lax.broadcasted_iota(jnp.int32, (bk, bk), 1)
    sg = sg_ref[...]
    ones_rows = jnp.ones((SUM_ROWS, bk), _BF16)
    all_groups = tuple(range(2 * nq))

    def parts_from(c0):
        return tuple(g for g in all_groups if g % nq >= c0)

    def scores(j, st, groups):
        mx = mx_a if st is st_a else mx_b
        kb = k_ref[pl.ds(pl.multiple_of(j * bk, bk), bk), :]
        for g in groups:
            r = jnp.dot(kb, qz_scr[g], preferred_element_type=_F32)
            st[g] = r
            mx[g] = jnp.max(r, axis=0, keepdims=True)

    def absorb(j, st, groups, masked_part):
        vt = jnp.concatenate([vt_ref[j], ones_rows], axis=0)
        for g in groups:
            s = st[g]
            if g % nq == masked_part:
                s = jnp.where(diag_ok, s, -jnp.inf)
                s_max = jnp.max(s, axis=0, keepdims=True)
            else:
                s_max = (mx_a if st is st_a else mx_b)[g]
            m_old = m_ref[g]
            m_new = jnp.maximum(m_old, s_max)
            alpha = jnp.exp2(m_old - m_new)
            p = jnp.exp2(s - m_new).astype(_BF16)
            m_ref[g] = m_new
            acc_ref[g] = acc_ref[g] * alpha + jnp.dot(vt, p, preferred_element_type=_F32)

    def load_queries(qi):
        q0 = pl.multiple_of(qi * bq, bq)
        qt = q_ref[pl.ds(q0, bq), :].astype(_F32).T
        for g in all_groups:
            qh = qt[:, (g % nq) * bk:(g % nq + 1) * bk]
            keep = (row < HEAD_DIM) if g // nq == 0 else (row >= HEAD_DIM)
            qz_scr[g] = jnp.where(keep, qh, 0.0).astype(_BF16)
        scores(0, st_a, all_groups)

    def reset_state():
        acc_ref[...] = jnp.zeros_like(acc_ref)
        m_ref[...] = jnp.full_like(m_ref, -jnp.inf)

    def visible_blocks(t, c):
        for r in range(nq):
            cur, nxt = (st_a, st_b) if r % 2 == 0 else (st_b, st_a)
            scores(nq * t + r + 1, nxt, all_groups[:LEAD])
            for i, g in enumerate(all_groups):
                scores(nq * t + r + 1, nxt, all_groups[i + LEAD:i + LEAD + 1])
                absorb(nq * t + r, cur, (g,), None)
        return c

    def normalized(g):
        acc = acc_ref[g]
        return acc[:V_HEAD_DIM] * (1.0 / acc[V_HEAD_DIM:V_HEAD_DIM + 1])

    n_qblocks = seq // bq

    def q_block(qi, carry):
        lax.fori_loop(0, qi, visible_blocks, 0)
        jd = nq * qi
        for r in range(nq):
            cur, nxt = (st_a, st_b) if r % 2 == 0 else (st_b, st_a)
            ahead = parts_from(r + 1)
            scores(jd + r + 1, nxt, ahead[:LEAD])
            for i, g in enumerate(parts_from(r)):
                scores(jd + r + 1, nxt, ahead[i + LEAD:i + LEAD + 1])
                absorb(jd + r, cur, (g,), r)

        load_queries(jnp.minimum(qi + 1, n_qblocks - 1))

        q0 = pl.multiple_of(qi * bq, bq)
        for c in range(nq):
            o = normalized(c) - lam * normalized(nq + c)
            on = o * lax.rsqrt(jnp.mean(o * o, axis=0, keepdims=True) + NORM_EPS)
            y = on.T * sg * (1.0 - lambda_init)
            rows = pl.ds(q0 + c * bk, bk)
            o_ref[rows, :] = (ga_ref[rows, :].astype(_F32) * y + gc_ref[rows, :].astype(_F32)).astype(o_ref.dtype)
        reset_state()
        return carry

    load_queries(0)
    reset_state()
    lax.fori_loop(0, n_qblocks, q_block, 0)


def _ffn_kernel(x_ref, mg_ref, wo_ref, g_ref, wu_ref, cw_ref, cb_ref, wd_ref, o_ref,
                h_scr, u_scr, *, tm, cf, n_chunks):
    d_ff = cf * n_chunks
    @pl.when(pl.program_id(1) == 0)
    def _():
        for c in range(n_chunks):
            u_scr[c, 0:HALO, :] = jnp.zeros((HALO, 2 * cf), _F32)

    x1 = x_ref[...] + jnp.dot(mg_ref[...], wo_ref[...], preferred_element_type=_F32)
    ms = jnp.mean(x1 * x1, axis=-1, keepdims=True)
    h_scr[...] = (x1 * lax.rsqrt(ms + NORM_EPS) * g_ref[...]).astype(_BF16)

    def ab_cols(ref, c):
        return jnp.concatenate([ref[:, c * cf:(c + 1) * cf], ref[:, d_ff + c * cf:d_ff + (c + 1) * cf]], axis=1)

    def up(c):
        return jnp.dot(h_scr[...], ab_cols(wu_ref, c), preferred_element_type=_F32)

    acc = x1
    u_next = up(0)
    for c in range(n_chunks):
        u = u_next
        if c + 1 < n_chunks:
            u_next = up(c + 1)
        u_scr[c, HALO:HALO + tm, :] = u
        cw = ab_cols(cw_ref, c)
        uc = (cw[2:3, :] * u + cw[1:2, :] * u_scr[c, HALO - 1:HALO - 1 + tm, :]
              + cw[0:1, :] * u_scr[c, HALO - 2:HALO - 2 + tm, :] + ab_cols(cb_ref, c))
        u_scr[c, 0:HALO, :] = u_scr[c, tm:tm + HALO, :]
        a = uc[:, :cf]
        act = (a * _sigmoid(a) * uc[:, cf:]).astype(_BF16)
        acc = acc + jnp.dot(act, wd_ref[c * cf:(c + 1) * cf, :], preferred_element_type=_F32)
    o_ref[...] = acc


def _const_spec(shape):
    nd = len(shape)
    return pl.BlockSpec(shape, lambda *_: (0,) * nd, pipeline_mode=pl.Buffered(1))


def _rope_tables(seq):
    inv = ROPE_THETA ** (-jnp.arange(0, HEAD_DIM, 2, dtype=_F32) / HEAD_DIM)
    ang = jnp.arange(seq, dtype=_F32)[:, None] * inv[None, :]
    c, s = jnp.cos(ang), jnp.sin(ang)
    return jnp.concatenate([c, c, c, c], axis=1), jnp.concatenate([-s, s, -s, s], axis=1)


def kernel(x, attn_norm_g, w_in, b_gate, q_norm_g, k_norm_g, lambda_q1, lambda_k1, lambda_q2, lambda_k2,
           subln_g, short_conv_w, w_out, ffn_norm_g, w_up, ffn_conv_w, ffn_conv_b, w_down):
    bsz, seq, d = x.shape
    depth = attn_norm_g.shape[0]
    d_ff = w_down.shape[1]
    assert d == N_HEADS * V_HEAD_DIM and w_in.shape[2] == 8 * d

    tm1 = min(512, seq)
    bk = 256
    nq = 4 if seq % (4 * bk) == 0 else 2
    bq = nq * bk
    tm3 = min(256, seq)
    cf = 256
    n_chunks = d_ff // cf
    assert seq % tm1 == 0 and seq % bq == 0 and seq % tm3 == 0 and d_ff % cf == 0 and tm1 % bk == 0

    cosp, sinp = _rope_tables(seq)
    seg = jnp.kron(jnp.eye(2 * LANES // HEAD_DIM, dtype=_F32), jnp.full((HEAD_DIM, HEAD_DIM), 1.0 / HEAD_DIM, _F32)).astype(_BF16)
    params = pltpu.CompilerParams(dimension_semantics=("arbitrary", "arbitrary"), vmem_limit_bytes=VMEM_LIMIT_BYTES)

    for l in range(depth):
        lambda_init = 0.8 - 0.6 * math.exp(-0.3 * l)
        qg = jnp.tile(q_norm_g[l] * (HEAD_DIM ** -0.5 * math.log2(math.e)), 2 * N_HEADS)[None, :]
        kg = jnp.tile(k_norm_g[l], 2 * N_HEADS)[None, :]

        act = jax.ShapeDtypeStruct((bsz, seq, d), _BF16)
        tok1 = pl.BlockSpec((None, tm1, d), lambda b, i: (b, i, 0))
        q, k, vt, ga, gc = pl.pallas_call(
            functools.partial(_proj_kernel, tm=tm1, bk=bk, d=d),
            grid=(bsz, seq // tm1),
            in_specs=[tok1, _const_spec((1, d)), _const_spec((d, 8 * d)), _const_spec((1, 2 * d)),
                      _const_spec((1, d)), _const_spec((1, d)),
                      pl.BlockSpec((tm1, LANES), lambda b, i: (i, 0)), pl.BlockSpec((tm1, LANES), lambda b, i: (i, 0)),
                      _const_spec((CONV_K, d)), _const_spec((2 * LANES, 2 * LANES))],
            out_specs=[tok1, tok1, pl.BlockSpec((None, tm1 // bk, d, bk), lambda b, i: (b, i, 0, 0)), tok1, tok1],
            out_shape=[act, act, jax.ShapeDtypeStruct((bsz, seq // bk, d, bk), _BF16), act, act],
            scratch_shapes=[pltpu.VMEM((tm1 + HALO, d), _F32)],
            compiler_params=params, name="norm_proj",
        )(x, attn_norm_g[l][None, :], w_in[l].astype(_BF16), b_gate[l][None, :], qg, kg, cosp, sinp,
          short_conv_w[l], seg)

        lamv = jnp.stack([lambda_q1[l], lambda_k1[l], lambda_q2[l], lambda_k2[l]]).astype(_F32)
        head = pl.BlockSpec((None, seq, V_HEAD_DIM), lambda b, hd: (b, 0, hd))
        merged = pl.pallas_call(
            functools.partial(_attn_kernel, seq=seq, bk=bk, nq=nq, lambda_init=lambda_init),
            grid=(bsz, N_HEADS),
            in_specs=[_const_spec((4, HEAD_DIM)), head, head,
                      pl.BlockSpec((None, seq // bk, V_HEAD_DIM, bk), lambda b, hd: (b, 0, hd, 0)),
                      head, head, _const_spec((1, V_HEAD_DIM))],
            out_specs=head,
            out_shape=act,
            scratch_shapes=[pltpu.VMEM((2 * nq, V_HEAD_DIM, bk), _BF16),
                            pltpu.VMEM((2 * nq, bk, bk), _F32), pltpu.VMEM((2 * nq, bk, bk), _F32),
                            pltpu.VMEM((2 * nq, 1, bk), _F32), pltpu.VMEM((2 * nq, 1, bk), _F32),
                            pltpu.VMEM((2 * nq, V_HEAD_DIM + SUM_ROWS, bk), _F32), pltpu.VMEM((2 * nq, 1, bk), _F32)],
            compiler_params=params, name="diff_attn",
        )(lamv, q, k, vt, ga, gc, subln_g[l][None, :])

        tok3 = pl.BlockSpec((None, tm3, d), lambda b, i: (b, i, 0))
        x = pl.pallas_call(
            functools.partial(_ffn_kernel, tm=tm3, cf=cf, n_chunks=n_chunks),
            grid=(bsz, seq // tm3),
            in_specs=[tok3, tok3, _const_spec((d, d)), _const_spec((1, d)), _const_spec((d, 2 * d_ff)),
                      _const_spec((CONV_K, 2 * d_ff)), _const_spec((1, 2 * d_ff)), _const_spec((d_ff, d))],
            out_specs=tok3,
            out_shape=jax.ShapeDtypeStruct((bsz, seq, d), x.dtype),
            scratch_shapes=[pltpu.VMEM((tm3, d), _BF16), pltpu.VMEM((n_chunks, tm3 + HALO, 2 * cf), _F32)],
            compiler_params=params, name="out_proj_ffn",
        )(x, merged, w_out[l].astype(_BF16), ffn_norm_g[l][None, :], w_up[l].astype(_BF16), ffn_conv_w[l],
          ffn_conv_b[l][None, :], w_down[l].astype(_BF16))
    return x
```

```python
import functools
import math

import jax
import jax.numpy as jnp
from jax import lax
from jax.experimental import pallas as pl
from jax.experimental.pallas import tpu as pltpu

N_HEADS = 8
HEAD_DIM = 64
V_HEAD_DIM = 2 * HEAD_DIM
ROPE_THETA = 10000.0
NORM_EPS = 1e-6
CONV_K = 3
HALO = 8
LANES = 128
SUM_ROWS = 16
LEAD = 1
VMEM_LIMIT_BYTES = 56 * 1024 * 1024

_F32 = jnp.float32
_BF16 = jnp.bfloat16


def _sigmoid(x):
    return 1.0 / (1.0 + jnp.exp(-x))


def _proj_kernel(x_ref, g_ref, w_ref, bg_ref, qg_ref, kg_ref, cos_ref, sin_ref, scw_ref, seg_ref,
                 q_ref, k_ref, vt_ref, ga_ref, gc_ref, z_scr, *, tm, bk, d):
    x = x_ref[...]
    ms = jnp.mean(x * x, axis=-1, keepdims=True)
    h = (x * lax.rsqrt(ms + NORM_EPS) * g_ref[...]).astype(_BF16)

    cw = 2 * LANES
    n_pieces = d // cw

    def proj(g, c):
        col = g * d + c * cw
        return jnp.dot(h, w_ref[:, col:col + cw], preferred_element_type=_F32)

    cosp = cos_ref[...]
    sinp = sin_ref[...]
    seg = seg_ref[...]
    lane = lax.broadcasted_iota(jnp.int32, (tm, LANES), 1)
    first_half = (lane & (HEAD_DIM // 2)) == 0
    first_tile = pl.program_id(1) == 0

    @pl.when(first_tile)
    def _():
        z_scr[0:HALO, :] = jnp.zeros((HALO, d), _F32)

    def norm_rope(pc, gam_ref, o_ref, c):
        msq = jnp.dot((pc * pc).astype(_BF16), seg, preferred_element_type=_F32)
        xn = pc * lax.rsqrt(msq + NORM_EPS) * gam_ref[:, c * cw:(c + 1) * cw]
        for hh in range(cw // LANES):
            xh = xn[:, hh * LANES:(hh + 1) * LANES]
            rot = jnp.where(first_half, pltpu.roll(xh, LANES - HEAD_DIM // 2, 1), pltpu.roll(xh, HEAD_DIM // 2, 1))
            col = c * cw + hh * LANES
            o_ref[:, col:col + LANES] = (xh * cosp + rot * sinp).astype(o_ref.dtype)

    def q_piece(ps, c):
        norm_rope(ps[0], qg_ref, q_ref, c)

    def k_piece(ps, c):
        norm_rope(ps[0], kg_ref, k_ref, c)

    def v_piece(ps, c):
        for r in range(tm // bk):
            vt_ref[r, c * cw:(c + 1) * cw, :] = ps[0][r * bk:(r + 1) * bk, :].T.astype(vt_ref.dtype)

    def conv_piece(ps, c):
        p_b, p_c, p_x, p_g = ps
        cols = slice(c * cw, (c + 1) * cw)
        z = p_c * p_x
        w = scw_ref[:, cols]
        z_scr[HALO:HALO + tm, cols] = z
        conv = (w[2:3, :] * z + w[1:2, :] * z_scr[HALO - 1:HALO - 1 + tm, cols]
                + w[0:1, :] * z_scr[HALO - 2:HALO - 2 + tm, cols])
        z_scr[0:HALO, cols] = z_scr[tm:tm + HALO, cols]
        gate = _sigmoid(p_g + bg_ref[:, d + c * cw:d + (c + 1) * cw])
        gc_ref[:, cols] = (gate * (p_b * conv)).astype(gc_ref.dtype)

    def ga_piece(ps, c):
        cols = slice(c * cw, (c + 1) * cw)
        ga_ref[:, cols] = _sigmoid(ps[0] + bg_ref[:, cols]).astype(ga_ref.dtype)

    work = [((0,), q_piece), ((1,), k_piece), ((2,), v_piece), ((3, 4, 5, 7), conv_piece), ((6,), ga_piece)]
    units = [(groups, fn, c) for groups, fn in work for c in range(n_pieces)]
    pending = [proj(g, units[0][2]) for g in units[0][0]]
    for i, (groups, fn, c) in enumerate(units):
        ps = pending
        if i + 1 < len(units):
            pending = [proj(g, units[i + 1][2]) for g in units[i + 1][0]]
        fn(ps, c)


def _attn_kernel(lam_ref, q_ref, k_ref, vt_ref, ga_ref, gc_ref, sg_ref, o_ref,
                 qz_scr, st_a, st_b, st_c, st_d, mx_a, mx_b, mx_c, mx_d, acc_ref, m_ref, *, seq, bk, nq, lambda_init):
    bq = nq * bk
    lv = lam_ref[...]
    lam = (jnp.exp(jnp.sum(lv[0:1] * lv[1:2], axis=-1, keepdims=True))
           - jnp.exp(jnp.sum(lv[2:3] * lv[3:4], axis=-1, keepdims=True)) + lambda_init)
    row = lax.broadcasted_iota(jnp.int32, (V_HEAD_DIM, bk), 0)
    diag_ok = lax.broadcasted_iota(jnp.int32, (bk, bk), 0) <= lax.broadcasted_iota(jnp.int32, (bk, bk), 1)
    sg = sg_ref[...]
    ones_rows = jnp.ones((SUM_ROWS, bk), _BF16)
    all_groups = tuple(range(2 * nq))
    pair_ab = ((st_a, mx_a), (st_b, mx_b))
    pair_cd = ((st_c, mx_c), (st_d, mx_d))

    def scores(j, buf, g):
        st, mx = buf
        kb = k_ref[pl.ds(pl.multiple_of(j * bk, bk), bk), :]
        r = jnp.dot(kb, qz_scr[g], preferred_element_type=_F32)
        st[g] = r
        mx[g] = jnp.max(r, axis=0, keepdims=True)

    def absorb(g, blocks):
        tiles, maxes = [], []
        for j, (st, mx), diagonal in blocks:
            s = st[g]
            if diagonal:
                s = jnp.where(diag_ok, s, -jnp.inf)
                maxes.append(jnp.max(s, axis=0, keepdims=True))
            else:
                maxes.append(mx[g])
            tiles.append(s)
        m_old = m_ref[g]
        m_new = functools.reduce(jnp.maximum, maxes, m_old)
        alpha = jnp.exp2(m_old - m_new)
        p = jnp.concatenate([jnp.exp2(s - m_new).astype(_BF16) for s in tiles], axis=0)
        vt = jnp.concatenate([jnp.concatenate([vt_ref[j], ones_rows], axis=0) for j, _, _ in blocks], axis=1)
        m_ref[g] = m_new
        acc_ref[g] = acc_ref[g] * alpha + jnp.dot(vt, p, preferred_element_type=_F32)

    def pair_step(j0, cur, nxt, visible, ahead):
        todo = [g for g in all_groups if ahead(g)]
        work = [g for g in all_groups if visible(g)]

        def issue(gs):
            for g in gs:
                for r in ahead(g):
                    scores(j0 + 2 + r, nxt[r], g)

        issue(todo[:LEAD])
        for i, g in enumerate(work):
            issue(todo[i + LEAD:i + LEAD + 1])
            absorb(g, [(j0 + r, cur[r], diagonal) for r, diagonal in visible(g)])
        issue(todo[len(work) + LEAD:])

    both_visible = lambda g: ((0, False), (1, False))
    both_ahead = lambda g: (0, 1)

    def load_queries(qi):
        q0 = pl.multiple_of(qi * bq, bq)
        qt = q_ref[pl.ds(q0, bq), :].astype(_F32).T
        for g in all_groups:
            qh = qt[:, (g % nq) * bk:(g % nq + 1) * bk]
            keep = (row < HEAD_DIM) if g // nq == 0 else (row >= HEAD_DIM)
            qz_scr[g] = jnp.where(keep, qh, 0.0).astype(_BF16)
        for g in all_groups:
            scores(0, pair_ab[0], g)
            scores(1, pair_ab[1], g)

    def reset_state():
        acc_ref[...] = jnp.zeros_like(acc_ref)
        m_ref[...] = jnp.full_like(m_ref, -jnp.inf)

    def visible_blocks(t, c):
        pair_step(4 * t, pair_ab, pair_cd, both_visible, both_ahead)
        pair_step(4 * t + 2, pair_cd, pair_ab, both_visible, both_ahead)
        return c

    def normalized(g):
        acc = acc_ref[g]
        return acc[:V_HEAD_DIM] * (1.0 / acc[V_HEAD_DIM:V_HEAD_DIM + 1])

    n_qblocks = seq // bq

    def diagonal_view(first):
        def view(g):
            c = g % nq
            return tuple((r - first, r == c) for r in (first, first + 1) if r <= c)
        return view

    def q_block(qi, carry):
        lax.fori_loop(0, qi, visible_blocks, 0)
        jd = nq * qi
        seen_ab, seen_cd = diagonal_view(0), diagonal_view(2)
        pair_step(jd, pair_ab, pair_cd, seen_ab, lambda g: tuple(r for r, _ in seen_cd(g)))
        pair_step(jd + 2, pair_cd, pair_ab, seen_cd, lambda g: ())

        load_queries(jnp.minimum(qi + 1, n_qblocks - 1))

        q0 = pl.multiple_of(qi * bq, bq)
        for c in range(nq):
            o = normalized(c) - lam * normalized(nq + c)
            on = o * lax.rsqrt(jnp.mean(o * o, axis=0, keepdims=True) + NORM_EPS)
            y = on.T * sg * (1.0 - lambda_init)
            rows = pl.ds(q0 + c * bk, bk)
            o_ref[rows, :] = (ga_ref[rows, :].astype(_F32) * y + gc_ref[rows, :].astype(_F32)).astype(o_ref.dtype)
        reset_state()
        return carry

    load_queries(0)
    reset_state()
    lax.fori_loop(0, n_qblocks, q_block, 0)


def _ffn_kernel(x_ref, mg_ref, wo_ref, g_ref, wu_ref, cw_ref, cb_ref, wd_ref, o_ref,
                h_scr, u_scr, *, tm, cf, n_chunks):
    d_ff = cf * n_chunks
    @pl.when(pl.program_id(1) == 0)
    def _():
        for c in range(n_chunks):
            u_scr[c, 0:HALO, :] = jnp.zeros((HALO, 2 * cf), _F32)

    x1 = x_ref[...] + jnp.dot(mg_ref[...], wo_ref[...], preferred_element_type=_F32)
    ms = jnp.mean(x1 * x1, axis=-1, keepdims=True)
    h_scr[...] = (x1 * lax.rsqrt(ms + NORM_EPS) * g_ref[...]).astype(_BF16)

    def ab_cols(ref, c):
        return jnp.concatenate([ref[:, c * cf:(c + 1) * cf], ref[:, d_ff + c * cf:d_ff + (c + 1) * cf]], axis=1)

    def up(c):
        return jnp.dot(h_scr[...], ab_cols(wu_ref, c), preferred_element_type=_F32)

    acc = x1
    u_next = up(0)
    for c in range(n_chunks):
        u = u_next
        if c + 1 < n_chunks:
            u_next = up(c + 1)
        u_scr[c, HALO:HALO + tm, :] = u
        cw = ab_cols(cw_ref, c)
        uc = (cw[2:3, :] * u + cw[1:2, :] * u_scr[c, HALO - 1:HALO - 1 + tm, :]
              + cw[0:1, :] * u_scr[c, HALO - 2:HALO - 2 + tm, :] + ab_cols(cb_ref, c))
        u_scr[c, 0:HALO, :] = u_scr[c, tm:tm + HALO, :]
        a = uc[:, :cf]
        act = (a * _sigmoid(a) * uc[:, cf:]).astype(_BF16)
        acc = acc + jnp.dot(act, wd_ref[c * cf:(c + 1) * cf, :], preferred_element_type=_F32)
    o_ref[...] = acc


def _const_spec(shape):
    nd = len(shape)
    return pl.BlockSpec(shape, lambda *_: (0,) * nd, pipeline_mode=pl.Buffered(1))


def _rope_tables(seq):
    inv = ROPE_THETA ** (-jnp.arange(0, HEAD_DIM, 2, dtype=_F32) / HEAD_DIM)
    ang = jnp.arange(seq, dtype=_F32)[:, None] * inv[None, :]
    c, s = jnp.cos(ang), jnp.sin(ang)
    return jnp.concatenate([c, c, c, c], axis=1), jnp.concatenate([-s, s, -s, s], axis=1)


def kernel(x, attn_norm_g, w_in, b_gate, q_norm_g, k_norm_g, lambda_q1, lambda_k1, lambda_q2, lambda_k2,
           subln_g, short_conv_w, w_out, ffn_norm_g, w_up, ffn_conv_w, ffn_conv_b, w_down):
    bsz, seq, d = x.shape
    depth = attn_norm_g.shape[0]
    d_ff = w_down.shape[1]
    assert d == N_HEADS * V_HEAD_DIM and w_in.shape[2] == 8 * d

    tm1 = min(512, seq)
    bk = 256
    nq = 4
    bq = nq * bk
    tm3 = min(256, seq)
    cf = 256
    n_chunks = d_ff // cf
    assert seq % tm1 == 0 and seq % bq == 0 and seq % tm3 == 0 and d_ff % cf == 0 and tm1 % bk == 0

    cosp, sinp = _rope_tables(seq)
    seg = jnp.kron(jnp.eye(2 * LANES // HEAD_DIM, dtype=_F32), jnp.full((HEAD_DIM, HEAD_DIM), 1.0 / HEAD_DIM, _F32)).astype(_BF16)
    params = pltpu.CompilerParams(dimension_semantics=("arbitrary", "arbitrary"), vmem_limit_bytes=VMEM_LIMIT_BYTES)

    for l in range(depth):
        lambda_init = 0.8 - 0.6 * math.exp(-0.3 * l)
        qg = jnp.tile(q_norm_g[l] * (HEAD_DIM ** -0.5 * math.log2(math.e)), 2 * N_HEADS)[None, :]
        kg = jnp.tile(k_norm_g[l], 2 * N_HEADS)[None, :]

        act = jax.ShapeDtypeStruct((bsz, seq, d), _BF16)
        tok1 = pl.BlockSpec((None, tm1, d), lambda b, i: (b, i, 0))
        q, k, vt, ga, gc = pl.pallas_call(
            functools.partial(_proj_kernel, tm=tm1, bk=bk, d=d),
            grid=(bsz, seq // tm1),
            in_specs=[tok1, _const_spec((1, d)), _const_spec((d, 8 * d)), _const_spec((1, 2 * d)),
                      _const_spec((1, d)), _const_spec((1, d)),
                      pl.BlockSpec((tm1, LANES), lambda b, i: (i, 0)), pl.BlockSpec((tm1, LANES), lambda b, i: (i, 0)),
                      _const_spec((CONV_K, d)), _const_spec((2 * LANES, 2 * LANES))],
            out_specs=[tok1, tok1, pl.BlockSpec((None, tm1 // bk, d, bk), lambda b, i: (b, i, 0, 0)), tok1, tok1],
            out_shape=[act, act, jax.ShapeDtypeStruct((bsz, seq // bk, d, bk), _BF16), act, act],
            scratch_shapes=[pltpu.VMEM((tm1 + HALO, d), _F32)],
            compiler_params=params, name="norm_proj",
        )(x, attn_norm_g[l][None, :], w_in[l].astype(_BF16), b_gate[l][None, :], qg, kg, cosp, sinp,
          short_conv_w[l], seg)

        lamv = jnp.stack([lambda_q1[l], lambda_k1[l], lambda_q2[l], lambda_k2[l]]).astype(_F32)
        head = pl.BlockSpec((None, seq, V_HEAD_DIM), lambda b, hd: (b, 0, hd))
        merged = pl.pallas_call(
            functools.partial(_attn_kernel, seq=seq, bk=bk, nq=nq, lambda_init=lambda_init),
            grid=(bsz, N_HEADS),
            in_specs=[_const_spec((4, HEAD_DIM)), head, head,
                      pl.BlockSpec((None, seq // bk, V_HEAD_DIM, bk), lambda b, hd: (b, 0, hd, 0)),
                      head, head, _const_spec((1, V_HEAD_DIM))],
            out_specs=head,
            out_shape=act,
            scratch_shapes=[pltpu.VMEM((2 * nq, V_HEAD_DIM, bk), _BF16),
                            *[pltpu.VMEM((2 * nq, bk, bk), _F32)] * 4, *[pltpu.VMEM((2 * nq, 1, bk), _F32)] * 4,
                            pltpu.VMEM((2 * nq, V_HEAD_DIM + SUM_ROWS, bk), _F32), pltpu.VMEM((2 * nq, 1, bk), _F32)],
            compiler_params=params, name="diff_attn",
        )(lamv, q, k, vt, ga, gc, subln_g[l][None, :])

        tok3 = pl.BlockSpec((None, tm3, d), lambda b, i: (b, i, 0))
        x = pl.pallas_call(
            functools.partial(_ffn_kernel, tm=tm3, cf=cf, n_chunks=n_chunks),
            grid=(bsz, seq // tm3),
            in_specs=[tok3, tok3, _const_spec((d, d)), _const_spec((1, d)), _const_spec((d, 2 * d_ff)),
                      _const_spec((CONV_K, 2 * d_ff)), _const_spec((1, 2 * d_ff)), _const_spec((d_ff, d))],
            out_specs=tok3,
            out_shape=jax.ShapeDtypeStruct((bsz, seq, d), x.dtype),
            scratch_shapes=[pltpu.VMEM((tm3, d), _BF16), pltpu.VMEM((n_chunks, tm3 + HALO, 2 * cf), _F32)],
            compiler_params=params, name="out_proj_ffn",
        )(x, merged, w_out[l].astype(_BF16), ffn_norm_g[l][None, :], w_up[l].astype(_BF16), ffn_conv_w[l],
          ffn_conv_b[l][None, :], w_down[l].astype(_BF16))
    return x
```

```python
import functools
import math

import jax
import jax.numpy as jnp
from jax import lax
from jax.experimental import pallas as pl
from jax.experimental.pallas import tpu as pltpu

N_HEADS = 8
HEAD_DIM = 64
V_HEAD_DIM = 2 * HEAD_DIM
ROPE_THETA = 10000.0
NORM_EPS = 1e-6
CONV_K = 3
HALO = 8
LANES = 128
SUM_ROWS = 16
LEAD = 1
VMEM_LIMIT_BYTES = 56 * 1024 * 1024

_F32 = jnp.float32
_BF16 = jnp.bfloat16


def _sigmoid(x):
    return 1.0 / (1.0 + jnp.exp(-x))


def _proj_kernel(x_ref, g_ref, w_ref, bg_ref, qg_ref, kg_ref, cos_ref, sin_ref, scw_ref, seg_ref,
                 q_ref, k_ref, vt_ref, ga_ref, gc_ref, z_scr, *, tm, n_sub, bk, d):
    cw = 2 * LANES
    n_pieces = d // cw
    seg = seg_ref[...]
    lane = lax.broadcasted_iota(jnp.int32, (tm, LANES), 1)
    first_half = (lane & (HEAD_DIM // 2)) == 0

    @pl.when(pl.program_id(1) == 0)
    def _():
        z_scr[0:HALO, :] = jnp.zeros((HALO, d), _F32)

    def sub_tile(i, carry):
        rows = pl.ds(pl.multiple_of(i * tm, tm), tm)
        x = x_ref[rows, :]
        ms = jnp.mean(x * x, axis=-1, keepdims=True)
        h = (x * lax.rsqrt(ms + NORM_EPS) * g_ref[...]).astype(_BF16)
        cosp = cos_ref[rows, :]
        sinp = sin_ref[rows, :]

        def proj(g, c):
            col = g * d + c * cw
            return jnp.dot(h, w_ref[:, col:col + cw], preferred_element_type=_F32)

        def norm_rope(pc, gam_ref, o_ref, c):
            msq = jnp.dot((pc * pc).astype(_BF16), seg, preferred_element_type=_F32)
            xn = pc * lax.rsqrt(msq + NORM_EPS) * gam_ref[:, c * cw:(c + 1) * cw]
            for hh in range(cw // LANES):
                xh = xn[:, hh * LANES:(hh + 1) * LANES]
                rot = jnp.where(first_half, pltpu.roll(xh, LANES - HEAD_DIM // 2, 1), pltpu.roll(xh, HEAD_DIM // 2, 1))
                col = c * cw + hh * LANES
                o_ref[rows, col:col + LANES] = (xh * cosp + rot * sinp).astype(o_ref.dtype)

        def q_piece(ps, c):
            norm_rope(ps[0], qg_ref, q_ref, c)

        def k_piece(ps, c):
            norm_rope(ps[0], kg_ref, k_ref, c)

        def v_piece(ps, c):
            for r in range(tm // bk):
                vt_ref[i * (tm // bk) + r, c * cw:(c + 1) * cw, :] = ps[0][r * bk:(r + 1) * bk, :].T.astype(vt_ref.dtype)

        def conv_piece(ps, c):
            p_b, p_c, p_x, p_g = ps
            cols = slice(c * cw, (c + 1) * cw)
            z = p_c * p_x
            w = scw_ref[:, cols]
            z_scr[HALO:HALO + tm, cols] = z
            conv = (w[2:3, :] * z + w[1:2, :] * z_scr[HALO - 1:HALO - 1 + tm, cols]
                    + w[0:1, :] * z_scr[HALO - 2:HALO - 2 + tm, cols])
            z_scr[0:HALO, cols] = z_scr[tm:tm + HALO, cols]
            gate = _sigmoid(p_g + bg_ref[:, d + c * cw:d + (c + 1) * cw])
            gc_ref[rows, cols] = (gate * (p_b * conv)).astype(gc_ref.dtype)

        def ga_piece(ps, c):
            cols = slice(c * cw, (c + 1) * cw)
            ga_ref[rows, cols] = _sigmoid(ps[0] + bg_ref[:, cols]).astype(ga_ref.dtype)

        work = [((0,), q_piece), ((1,), k_piece), ((2,), v_piece), ((3, 4, 5, 7), conv_piece), ((6,), ga_piece)]
        units = [(groups, fn, c) for groups, fn in work for c in range(n_pieces)]
        pending = [proj(g, units[0][2]) for g in units[0][0]]
        for u, (groups, fn, c) in enumerate(units):
            ps = pending
            if u + 1 < len(units):
                pending = [proj(g, units[u + 1][2]) for g in units[u + 1][0]]
            fn(ps, c)
        return carry

    lax.fori_loop(0, n_sub, sub_tile, 0)


def _attn_kernel(lam_ref, q_ref, k_ref, vt_ref, ga_ref, gc_ref, sg_ref, o_ref,
                 qz_scr, st_a, st_b, st_c, st_d, mx_a, mx_b, mx_c, mx_d, acc_ref, m_ref, *, seq, bk, nq, lambda_init):
    bq = nq * bk
    lv = lam_ref[...]
    lam = (jnp.exp(jnp.sum(lv[0:1] * lv[1:2], axis=-1, keepdims=True))
           - jnp.exp(jnp.sum(lv[2:3] * lv[3:4], axis=-1, keepdims=True)) + lambda_init)
    row = lax.broadcasted_iota(jnp.int32, (V_HEAD_DIM, bk), 0)
    diag_ok = lax.broadcasted_iota(jnp.int32, (bk, bk), 0) <= lax.broadcasted_iota(jnp.int32, (bk, bk), 1)
    sg = sg_ref[...]
    ones_rows = jnp.ones((SUM_ROWS, bk), _BF16)
    all_groups = tuple(range(2 * nq))
    pair_ab = ((st_a, mx_a), (st_b, mx_b))
    pair_cd = ((st_c, mx_c), (st_d, mx_d))

    def scores(j, buf, g):
        st, mx = buf
        kb = k_ref[pl.ds(pl.multiple_of(j * bk, bk), bk), :]
        r = jnp.dot(kb, qz_scr[g], preferred_element_type=_F32)
        st[g] = r
        mx[g] = jnp.max(r, axis=0, keepdims=True)

    def absorb(g, blocks):
        tiles, maxes = [], []
        for j, (st, mx), diagonal in blocks:
            s = st[g]
            if diagonal:
                s = jnp.where(diag_ok, s, -jnp.inf)
                maxes.append(jnp.max(s, axis=0, keepdims=True))
            else:
                maxes.append(mx[g])
            tiles.append(s)
        m_old = m_ref[g]
        m_new = functools.reduce(jnp.maximum, maxes, m_old)
        alpha = jnp.exp2(m_old - m_new)
        p = jnp.concatenate([jnp.exp2(s - m_new).astype(_BF16) for s in tiles], axis=0)
        vt = jnp.concatenate([jnp.concatenate([vt_ref[j], ones_rows], axis=0) for j, _, _ in blocks], axis=1)
        m_ref[g] = m_new
        acc_ref[g] = acc_ref[g] * alpha + jnp.dot(vt, p, preferred_element_type=_F32)

    def pair_step(j0, cur, nxt, visible, ahead):
        todo = [g for g in all_groups if ahead(g)]
        work = [g for g in all_groups if visible(g)]

        def issue(gs):
            for g in gs:
                for r in ahead(g):
                    scores(j0 + 2 + r, nxt[r], g)

        issue(todo[:LEAD])
        for i, g in enumerate(work):
            issue(todo[i + LEAD:i + LEAD + 1])
            absorb(g, [(j0 + r, cur[r], diagonal) for r, diagonal in visible(g)])
        issue(todo[len(work) + LEAD:])

    both_visible = lambda g: ((0, False), (1, False))
    both_ahead = lambda g: (0, 1)

    def load_queries(qi):
        q0 = pl.multiple_of(qi * bq, bq)
        qt = q_ref[pl.ds(q0, bq), :].astype(_F32).T
        for g in all_groups:
            qh = qt[:, (g % nq) * bk:(g % nq + 1) * bk]
            keep = (row < HEAD_DIM) if g // nq == 0 else (row >= HEAD_DIM)
            qz_scr[g] = jnp.where(keep, qh, 0.0).astype(_BF16)
        for g in all_groups:
            scores(0, pair_ab[0], g)
            scores(1, pair_ab[1], g)

    def reset_state():
        acc_ref[...] = jnp.zeros_like(acc_ref)
        m_ref[...] = jnp.full_like(m_ref, -jnp.inf)

    def visible_blocks(t, c):
        pair_step(4 * t, pair_ab, pair_cd, both_visible, both_ahead)
        pair_step(4 * t + 2, pair_cd, pair_ab, both_visible, both_ahead)
        return c

    def normalized(g):
        acc = acc_ref[g]
        return acc[:V_HEAD_DIM] * (1.0 / acc[V_HEAD_DIM:V_HEAD_DIM + 1])

    n_qblocks = seq // bq

    def diagonal_view(first):
        def view(g):
            c = g % nq
            return tuple((r - first, r == c) for r in (first, first + 1) if r <= c)
        return view

    def q_block(qi, carry):
        lax.fori_loop(0, qi, visible_blocks, 0)
        jd = nq * qi
        seen_ab, seen_cd = diagonal_view(0), diagonal_view(2)
        pair_step(jd, pair_ab, pair_cd, seen_ab, lambda g: tuple(r for r, _ in seen_cd(g)))
        pair_step(jd + 2, pair_cd, pair_ab, seen_cd, lambda g: ())

        load_queries(jnp.minimum(qi + 1, n_qblocks - 1))

        q0 = pl.multiple_of(qi * bq, bq)
        for c in range(nq):
            o = normalized(c) - lam * normalized(nq + c)
            on = o * lax.rsqrt(jnp.mean(o * o, axis=0, keepdims=True) + NORM_EPS)
            y = on.T * sg * (1.0 - lambda_init)
            rows = pl.ds(q0 + c * bk, bk)
            o_ref[rows, :] = (ga_ref[rows, :].astype(_F32) * y + gc_ref[rows, :].astype(_F32)).astype(o_ref.dtype)
        reset_state()
        return carry

    load_queries(0)
    reset_state()
    lax.fori_loop(0, n_qblocks, q_block, 0)


def _ffn_kernel(x_ref, mg_ref, wo_ref, g_ref, wu_ref, cw_ref, cb_ref, wd_ref, o_ref,
                h_scr, u_scr, *, tm, n_sub, cf, n_chunks):
    d_ff = cf * n_chunks

    @pl.when(pl.program_id(1) == 0)
    def _():
        for c in range(n_chunks):
            u_scr[c, 0:HALO, :] = jnp.zeros((HALO, 2 * cf), _F32)

    def ab_cols(ref, c):
        return jnp.concatenate([ref[:, c * cf:(c + 1) * cf], ref[:, d_ff + c * cf:d_ff + (c + 1) * cf]], axis=1)

    def up(c):
        return jnp.dot(h_scr[...], ab_cols(wu_ref, c), preferred_element_type=_F32)

    def sub_tile(i, carry):
        rows = pl.ds(pl.multiple_of(i * tm, tm), tm)
        x1 = x_ref[rows, :] + jnp.dot(mg_ref[rows, :], wo_ref[...], preferred_element_type=_F32)
        ms = jnp.mean(x1 * x1, axis=-1, keepdims=True)
        h_scr[...] = (x1 * lax.rsqrt(ms + NORM_EPS) * g_ref[...]).astype(_BF16)

        acc = x1
        u_next = up(0)
        for c in range(n_chunks):
            u = u_next
            if c + 1 < n_chunks:
                u_next = up(c + 1)
            u_scr[c, HALO:HALO + tm, :] = u
            cw = ab_cols(cw_ref, c)
            uc = (cw[2:3, :] * u + cw[1:2, :] * u_scr[c, HALO - 1:HALO - 1 + tm, :]
                  + cw[0:1, :] * u_scr[c, HALO - 2:HALO - 2 + tm, :] + ab_cols(cb_ref, c))
            u_scr[c, 0:HALO, :] = u_scr[c, tm:tm + HALO, :]
            a = uc[:, :cf]
            act = (a * _sigmoid(a) * uc[:, cf:]).astype(_BF16)
            acc = acc + jnp.dot(act, wd_ref[c * cf:(c + 1) * cf, :], preferred_element_type=_F32)
        o_ref[rows, :] = acc
        return carry

    lax.fori_loop(0, n_sub, sub_tile, 0)


def _const_spec(shape):
    nd = len(shape)
    return pl.BlockSpec(shape, lambda *_: (0,) * nd, pipeline_mode=pl.Buffered(1))


def _rope_tables(seq):
    inv = ROPE_THETA ** (-jnp.arange(0, HEAD_DIM, 2, dtype=_F32) / HEAD_DIM)
    ang = jnp.arange(seq, dtype=_F32)[:, None] * inv[None, :]
    c, s = jnp.cos(ang), jnp.sin(ang)
    return jnp.concatenate([c, c, c, c], axis=1), jnp.concatenate([-s, s, -s, s], axis=1)


def kernel(x, attn_norm_g, w_in, b_gate, q_norm_g, k_norm_g, lambda_q1, lambda_k1, lambda_q2, lambda_k2,
           subln_g, short_conv_w, w_out, ffn_norm_g, w_up, ffn_conv_w, ffn_conv_b, w_down):
    bsz, seq, d = x.shape
    depth = attn_norm_g.shape[0]
    d_ff = w_down.shape[1]
    assert d == N_HEADS * V_HEAD_DIM and w_in.shape[2] == 8 * d

    tm1 = min(512, seq)
    n_sub1 = 2 if seq % (2 * tm1) == 0 else 1
    bk = 256
    nq = 4
    bq = nq * bk
    tm3 = min(256, seq)
    n_sub3 = 4 if seq % (4 * tm3) == 0 else 1
    cf = 256
    n_chunks = d_ff // cf
    assert seq % tm1 == 0 and seq % bq == 0 and seq % tm3 == 0 and d_ff % cf == 0 and tm1 % bk == 0

    cosp, sinp = _rope_tables(seq)
    seg = jnp.kron(jnp.eye(2 * LANES // HEAD_DIM, dtype=_F32), jnp.full((HEAD_DIM, HEAD_DIM), 1.0 / HEAD_DIM, _F32)).astype(_BF16)
    params = pltpu.CompilerParams(dimension_semantics=("arbitrary", "arbitrary"), vmem_limit_bytes=VMEM_LIMIT_BYTES)

    for l in range(depth):
        lambda_init = 0.8 - 0.6 * math.exp(-0.3 * l)
        qg = jnp.tile(q_norm_g[l] * (HEAD_DIM ** -0.5 * math.log2(math.e)), 2 * N_HEADS)[None, :]
        kg = jnp.tile(k_norm_g[l], 2 * N_HEADS)[None, :]

        act = jax.ShapeDtypeStruct((bsz, seq, d), _BF16)
        blk1 = n_sub1 * tm1
        tok1 = pl.BlockSpec((None, blk1, d), lambda b, i: (b, i, 0))
        q, k, vt, ga, gc = pl.pallas_call(
            functools.partial(_proj_kernel, tm=tm1, n_sub=n_sub1, bk=bk, d=d),
            grid=(bsz, seq // blk1),
            in_specs=[tok1, _const_spec((1, d)), _const_spec((d, 8 * d)), _const_spec((1, 2 * d)),
                      _const_spec((1, d)), _const_spec((1, d)),
                      pl.BlockSpec((blk1, LANES), lambda b, i: (i, 0)), pl.BlockSpec((blk1, LANES), lambda b, i: (i, 0)),
                      _const_spec((CONV_K, d)), _const_spec((2 * LANES, 2 * LANES))],
            out_specs=[tok1, tok1, pl.BlockSpec((None, blk1 // bk, d, bk), lambda b, i: (b, i, 0, 0)), tok1, tok1],
            out_shape=[act, act, jax.ShapeDtypeStruct((bsz, seq // bk, d, bk), _BF16), act, act],
            scratch_shapes=[pltpu.VMEM((tm1 + HALO, d), _F32)],
            compiler_params=params, name="norm_proj",
        )(x, attn_norm_g[l][None, :], w_in[l].astype(_BF16), b_gate[l][None, :], qg, kg, cosp, sinp,
          short_conv_w[l], seg)

        lamv = jnp.stack([lambda_q1[l], lambda_k1[l], lambda_q2[l], lambda_k2[l]]).astype(_F32)
        head = pl.BlockSpec((None, seq, V_HEAD_DIM), lambda b, hd: (b, 0, hd))
        merged = pl.pallas_call(
            functools.partial(_attn_kernel, seq=seq, bk=bk, nq=nq, lambda_init=lambda_init),
            grid=(bsz, N_HEADS),
            in_specs=[_const_spec((4, HEAD_DIM)), head, head,
                      pl.BlockSpec((None, seq // bk, V_HEAD_DIM, bk), lambda b, hd: (b, 0, hd, 0)),
                      head, head, _const_spec((1, V_HEAD_DIM))],
            out_specs=head,
            out_shape=act,
            scratch_shapes=[pltpu.VMEM((2 * nq, V_HEAD_DIM, bk), _BF16),
                            *[pltpu.VMEM((2 * nq, bk, bk), _F32)] * 4, *[pltpu.VMEM((2 * nq, 1, bk), _F32)] * 4,
                            pltpu.VMEM((2 * nq, V_HEAD_DIM + SUM_ROWS, bk), _F32), pltpu.VMEM((2 * nq, 1, bk), _F32)],
            compiler_params=params, name="diff_attn",
        )(lamv, q, k, vt, ga, gc, subln_g[l][None, :])

        tok3 = pl.BlockSpec((None, n_sub3 * tm3, d), lambda b, i: (b, i, 0))
        x = pl.pallas_call(
            functools.partial(_ffn_kernel, tm=tm3, n_sub=n_sub3, cf=cf, n_chunks=n_chunks),
            grid=(bsz, seq // (n_sub3 * tm3)),
            in_specs=[tok3, tok3, _const_spec((d, d)), _const_spec((1, d)), _const_spec((d, 2 * d_ff)),
                      _const_spec((CONV_K, 2 * d_ff)), _const_spec((1, 2 * d_ff)), _const_spec((d_ff, d))],
            out_specs=tok3,
            out_shape=jax.ShapeDtypeStruct((bsz, seq, d), x.dtype),
            scratch_shapes=[pltpu.VMEM((tm3, d), _BF16), pltpu.VMEM((n_chunks, tm3 + HALO, 2 * cf), _F32)],
            compiler_params=params, name="out_proj_ffn",
        )(x, merged, w_out[l].astype(_BF16), ffn_norm_g[l][None, :], w_up[l].astype(_BF16), ffn_conv_w[l],
          ffn_conv_b[l][None, :], w_down[l].astype(_BF16))
    return x
```

```python
import functools
import math

import jax
import jax.numpy as jnp
from jax import lax
from jax.experimental import pallas as pl
from jax.experimental.pallas import tpu as pltpu

N_HEADS = 8
HEAD_DIM = 64
V_HEAD_DIM = 2 * HEAD_DIM
ROPE_THETA = 10000.0
NORM_EPS = 1e-6
CONV_K = 3
HALO = 8
LANES = 128
SUM_ROWS = 16
LEAD = 1
VMEM_LIMIT_BYTES = 56 * 1024 * 1024

_F32 = jnp.float32
_BF16 = jnp.bfloat16


def _sigmoid(x):
    return 1.0 / (1.0 + jnp.exp(-x))


def _proj_kernel(x_ref, g_ref, w_ref, bg_ref, qg_ref, kg_ref, cos_ref, sin_ref, scw_ref, seg_ref,
                 q_ref, k_ref, vt_ref, ga_ref, gc_ref, z_scr, *, tm, n_sub, bk, d):
    cw = 2 * LANES
    n_pieces = d // cw
    seg = seg_ref[...]
    lane = lax.broadcasted_iota(jnp.int32, (tm, LANES), 1)
    first_half = (lane & (HEAD_DIM // 2)) == 0

    @pl.when(pl.program_id(1) == 0)
    def _():
        z_scr[0:HALO, :] = jnp.zeros((HALO, d), _F32)

    def sub_tile(i, carry):
        rows = pl.ds(pl.multiple_of(i * tm, tm), tm)
        x = x_ref[rows, :]
        ms = jnp.mean(x * x, axis=-1, keepdims=True)
        h = (x * lax.rsqrt(ms + NORM_EPS) * g_ref[...]).astype(_BF16)
        cosp = cos_ref[rows, :]
        sinp = sin_ref[rows, :]

        def proj(g, c):
            col = g * d + c * cw
            return jnp.dot(h, w_ref[:, col:col + cw], preferred_element_type=_F32)

        def norm_rope(pc, gam_ref, o_ref, c):
            msq = jnp.dot((pc * pc).astype(_BF16), seg, preferred_element_type=_F32)
            xn = pc * lax.rsqrt(msq + NORM_EPS) * gam_ref[:, c * cw:(c + 1) * cw]
            for hh in range(cw // LANES):
                xh = xn[:, hh * LANES:(hh + 1) * LANES]
                rot = jnp.where(first_half, pltpu.roll(xh, LANES - HEAD_DIM // 2, 1), pltpu.roll(xh, HEAD_DIM // 2, 1))
                col = c * cw + hh * LANES
                o_ref[rows, col:col + LANES] = (xh * cosp + rot * sinp).astype(o_ref.dtype)

        def q_piece(ps, c):
            norm_rope(ps[0], qg_ref, q_ref, c)

        def k_piece(ps, c):
            norm_rope(ps[0], kg_ref, k_ref, c)

        def v_piece(ps, c):
            for r in range(tm // bk):
                vt_ref[i * (tm // bk) + r, c * cw:(c + 1) * cw, :] = ps[0][r * bk:(r + 1) * bk, :].T.astype(vt_ref.dtype)

        def conv_piece(ps, c):
            p_b, p_c, p_x, p_g = ps
            cols = slice(c * cw, (c + 1) * cw)
            z = p_c * p_x
            w = scw_ref[:, cols]
            z_scr[HALO:HALO + tm, cols] = z
            conv = (w[2:3, :] * z + w[1:2, :] * z_scr[HALO - 1:HALO - 1 + tm, cols]
                    + w[0:1, :] * z_scr[HALO - 2:HALO - 2 + tm, cols])
            z_scr[0:HALO, cols] = z_scr[tm:tm + HALO, cols]
            gate = _sigmoid(p_g + bg_ref[:, d + c * cw:d + (c + 1) * cw])
            gc_ref[rows, cols] = (gate * (p_b * conv)).astype(gc_ref.dtype)

        def ga_piece(ps, c):
            cols = slice(c * cw, (c + 1) * cw)
            ga_ref[rows, cols] = _sigmoid(ps[0] + bg_ref[:, cols]).astype(ga_ref.dtype)

        work = [((0,), q_piece), ((1,), k_piece), ((2,), v_piece), ((3, 4, 5, 7), conv_piece), ((6,), ga_piece)]
        units = [(groups, fn, c) for groups, fn in work for c in range(n_pieces)]
        pending = [proj(g, units[0][2]) for g in units[0][0]]
        for u, (groups, fn, c) in enumerate(units):
            ps = pending
            if u + 1 < len(units):
                pending = [proj(g, units[u + 1][2]) for g in units[u + 1][0]]
            fn(ps, c)
        return carry

    lax.fori_loop(0, n_sub, sub_tile, 0)


def _attn_kernel(lam_ref, q_ref, k_ref, vt_ref, ga_ref, gc_ref, sg_ref, o_ref,
                 qz_scr, st_a, st_b, st_c, st_d, mx_a, mx_b, mx_c, mx_d, acc_ref, m_ref, *, seq, bk, nq, lambda_init):
    bq = nq * bk
    lv = lam_ref[...]
    lam = (jnp.exp(jnp.sum(lv[0:1] * lv[1:2], axis=-1, keepdims=True))
           - jnp.exp(jnp.sum(lv[2:3] * lv[3:4], axis=-1, keepdims=True)) + lambda_init)
    row = lax.broadcasted_iota(jnp.int32, (V_HEAD_DIM, bk), 0)
    diag_ok = lax.broadcasted_iota(jnp.int32, (bk, bk), 0) <= lax.broadcasted_iota(jnp.int32, (bk, bk), 1)
    sg = sg_ref[...]
    ones_rows = jnp.ones((SUM_ROWS, bk), _BF16)
    all_groups = tuple(range(2 * nq))
    pair_ab = ((st_a, mx_a), (st_b, mx_b))
    pair_cd = ((st_c, mx_c), (st_d, mx_d))

    def scores(j, buf, g):
        st, mx = buf
        kb = k_ref[pl.ds(pl.multiple_of(j * bk, bk), bk), :]
        r = jnp.dot(kb, qz_scr[g], preferred_element_type=_F32)
        st[g] = r
        mx[g] = jnp.max(r, axis=0, keepdims=True)

    def absorb(g, blocks):
        tiles, maxes = [], []
        for j, (st, mx), diagonal in blocks:
            s = st[g]
            if diagonal:
                s = jnp.where(diag_ok, s, -jnp.inf)
                maxes.append(jnp.max(s, axis=0, keepdims=True))
            else:
                maxes.append(mx[g])
            tiles.append(s)
        m_old = m_ref[g]
        m_new = functools.reduce(jnp.maximum, maxes, m_old)
        alpha = jnp.exp2(m_old - m_new)
        p = jnp.concatenate([jnp.exp2(s - m_new).astype(_BF16) for s in tiles], axis=0)
        vt = jnp.concatenate([jnp.concatenate([vt_ref[j], ones_rows], axis=0) for j, _, _ in blocks], axis=1)
        m_ref[g] = m_new
        acc_ref[g] = acc_ref[g] * alpha + jnp.dot(vt, p, preferred_element_type=_F32)

    def pair_step(j0, cur, nxt, visible, ahead):
        todo = [g for g in all_groups if ahead(g)]
        work = [g for g in all_groups if visible(g)]

        def issue(gs):
            for g in gs:
                for r in ahead(g):
                    scores(j0 + 2 + r, nxt[r], g)

        issue(todo[:LEAD])
        for i, g in enumerate(work):
            issue(todo[i + LEAD:i + LEAD + 1])
            absorb(g, [(j0 + r, cur[r], diagonal) for r, diagonal in visible(g)])
        issue(todo[len(work) + LEAD:])

    both_visible = lambda g: ((0, False), (1, False))
    both_ahead = lambda g: (0, 1)

    def load_queries(qi):
        q0 = pl.multiple_of(qi * bq, bq)
        qt = q_ref[pl.ds(q0, bq), :].astype(_F32).T
        for g in all_groups:
            qh = qt[:, (g % nq) * bk:(g % nq + 1) * bk]
            keep = (row < HEAD_DIM) if g // nq == 0 else (row >= HEAD_DIM)
            qz_scr[g] = jnp.where(keep, qh, 0.0).astype(_BF16)
        for g in all_groups:
            scores(0, pair_ab[0], g)
            scores(1, pair_ab[1], g)

    def reset_state():
        acc_ref[...] = jnp.zeros_like(acc_ref)
        m_ref[...] = jnp.full_like(m_ref, -jnp.inf)

    def visible_blocks(j0):
        pair_step(j0, pair_ab, pair_cd, both_visible, both_ahead)
        pair_step(j0 + 2, pair_cd, pair_ab, both_visible, both_ahead)

    def eight_blocks(t, c):
        visible_blocks(8 * t)
        visible_blocks(8 * t + 4)
        return c

    def four_blocks(j0):
        def body(t, c):
            visible_blocks(j0)
            return c
        return body

    def normalized(g):
        acc = acc_ref[g]
        return acc[:V_HEAD_DIM] * (1.0 / acc[V_HEAD_DIM:V_HEAD_DIM + 1])

    n_qblocks = seq // bq

    def diagonal_view(first):
        def view(g):
            c = g % nq
            return tuple((r - first, r == c) for r in (first, first + 1) if r <= c)
        return view

    def q_block(qi, carry):
        lax.fori_loop(0, qi // 2, eight_blocks, 0)
        lax.fori_loop(0, qi % 2, four_blocks(8 * (qi // 2)), 0)
        jd = nq * qi
        seen_ab, seen_cd = diagonal_view(0), diagonal_view(2)
        pair_step(jd, pair_ab, pair_cd, seen_ab, lambda g: tuple(r for r, _ in seen_cd(g)))
        pair_step(jd + 2, pair_cd, pair_ab, seen_cd, lambda g: ())

        load_queries(jnp.minimum(qi + 1, n_qblocks - 1))

        q0 = pl.multiple_of(qi * bq, bq)
        for c in range(nq):
            o = normalized(c) - lam * normalized(nq + c)
            on = o * lax.rsqrt(jnp.mean(o * o, axis=0, keepdims=True) + NORM_EPS)
            y = on.T * sg * (1.0 - lambda_init)
            rows = pl.ds(q0 + c * bk, bk)
            o_ref[rows, :] = (ga_ref[rows, :].astype(_F32) * y + gc_ref[rows, :].astype(_F32)).astype(o_ref.dtype)
        reset_state()
        return carry

    load_queries(0)
    reset_state()
    lax.fori_loop(0, n_qblocks, q_block, 0)


def _ffn_kernel(x_ref, mg_ref, wo_ref, g_ref, wu_ref, cw_ref, cb_ref, wd_ref, o_ref,
                h_scr, u_scr, *, tm, n_sub, cf, n_chunks):
    d_ff = cf * n_chunks

    @pl.when(pl.program_id(1) == 0)
    def _():
        for c in range(n_chunks):
            u_scr[c, 0:HALO, :] = jnp.zeros((HALO, 2 * cf), _F32)

    def ab_cols(ref, c):
        return jnp.concatenate([ref[:, c * cf:(c + 1) * cf], ref[:, d_ff + c * cf:d_ff + (c + 1) * cf]], axis=1)

    def up(c):
        return jnp.dot(h_scr[...], ab_cols(wu_ref, c), preferred_element_type=_F32)

    def sub_tile(i, carry):
        rows = pl.ds(pl.multiple_of(i * tm, tm), tm)
        x1 = x_ref[rows, :] + jnp.dot(mg_ref[rows, :], wo_ref[...], preferred_element_type=_F32)
        ms = jnp.mean(x1 * x1, axis=-1, keepdims=True)
        h_scr[...] = (x1 * lax.rsqrt(ms + NORM_EPS) * g_ref[...]).astype(_BF16)

        acc = x1
        u_next = up(0)
        for c in range(n_chunks):
            u = u_next
            if c + 1 < n_chunks:
                u_next = up(c + 1)
            u_scr[c, HALO:HALO + tm, :] = u
            cw = ab_cols(cw_ref, c)
            uc = (cw[2:3, :] * u + cw[1:2, :] * u_scr[c, HALO - 1:HALO - 1 + tm, :]
                  + cw[0:1, :] * u_scr[c, HALO - 2:HALO - 2 + tm, :] + ab_cols(cb_ref, c))
            u_scr[c, 0:HALO, :] = u_scr[c, tm:tm + HALO, :]
            a = uc[:, :cf]
            act = (a * _sigmoid(a) * uc[:, cf:]).astype(_BF16)
            acc = acc + jnp.dot(act, wd_ref[c * cf:(c + 1) * cf, :], preferred_element_type=_F32)
        o_ref[rows, :] = acc
        return carry

    lax.fori_loop(0, n_sub, sub_tile, 0)


def _const_spec(shape):
    nd = len(shape)
    return pl.BlockSpec(shape, lambda *_: (0,) * nd, pipeline_mode=pl.Buffered(1))


def _rope_tables(seq):
    inv = ROPE_THETA ** (-jnp.arange(0, HEAD_DIM, 2, dtype=_F32) / HEAD_DIM)
    ang = jnp.arange(seq, dtype=_F32)[:, None] * inv[None, :]
    c, s = jnp.cos(ang), jnp.sin(ang)
    return jnp.concatenate([c, c, c, c], axis=1), jnp.concatenate([-s, s, -s, s], axis=1)


def kernel(x, attn_norm_g, w_in, b_gate, q_norm_g, k_norm_g, lambda_q1, lambda_k1, lambda_q2, lambda_k2,
           subln_g, short_conv_w, w_out, ffn_norm_g, w_up, ffn_conv_w, ffn_conv_b, w_down):
    bsz, seq, d = x.shape
    depth = attn_norm_g.shape[0]
    d_ff = w_down.shape[1]
    assert d == N_HEADS * V_HEAD_DIM and w_in.shape[2] == 8 * d

    tm1 = min(512, seq)
    n_sub1 = 2 if seq % (2 * tm1) == 0 else 1
    bk = 256
    nq = 4
    bq = nq * bk
    tm3 = min(256, seq)
    n_sub3 = 4 if seq % (4 * tm3) == 0 else 1
    cf = 256
    n_chunks = d_ff // cf
    assert seq % tm1 == 0 and seq % bq == 0 and seq % tm3 == 0 and d_ff % cf == 0 and tm1 % bk == 0

    cosp, sinp = _rope_tables(seq)
    seg = jnp.kron(jnp.eye(2 * LANES // HEAD_DIM, dtype=_F32), jnp.full((HEAD_DIM, HEAD_DIM), 1.0 / HEAD_DIM, _F32)).astype(_BF16)
    params = pltpu.CompilerParams(dimension_semantics=("arbitrary", "arbitrary"), vmem_limit_bytes=VMEM_LIMIT_BYTES)

    for l in range(depth):
        lambda_init = 0.8 - 0.6 * math.exp(-0.3 * l)
        qg = jnp.tile(q_norm_g[l] * (HEAD_DIM ** -0.5 * math.log2(math.e)), 2 * N_HEADS)[None, :]
        kg = jnp.tile(k_norm_g[l], 2 * N_HEADS)[None, :]

        act = jax.ShapeDtypeStruct((bsz, seq, d), _BF16)
        blk1 = n_sub1 * tm1
        tok1 = pl.BlockSpec((None, blk1, d), lambda b, i: (b, i, 0))
        q, k, vt, ga, gc = pl.pallas_call(
            functools.partial(_proj_kernel, tm=tm1, n_sub=n_sub1, bk=bk, d=d),
            grid=(bsz, seq // blk1),
            in_specs=[tok1, _const_spec((1, d)), _const_spec((d, 8 * d)), _const_spec((1, 2 * d)),
                      _const_spec((1, d)), _const_spec((1, d)),
                      pl.BlockSpec((blk1, LANES), lambda b, i: (i, 0)), pl.BlockSpec((blk1, LANES), lambda b, i: (i, 0)),
                      _const_spec((CONV_K, d)), _const_spec((2 * LANES, 2 * LANES))],
            out_specs=[tok1, tok1, pl.BlockSpec((None, blk1 // bk, d, bk), lambda b, i: (b, i, 0, 0)), tok1, tok1],
            out_shape=[act, act, jax.ShapeDtypeStruct((bsz, seq // bk, d, bk), _BF16), act, act],
            scratch_shapes=[pltpu.VMEM((tm1 + HALO, d), _F32)],
            compiler_params=params, name="norm_proj",
        )(x, attn_norm_g[l][None, :], w_in[l].astype(_BF16), b_gate[l][None, :], qg, kg, cosp, sinp,
          short_conv_w[l], seg)

        lamv = jnp.stack([lambda_q1[l], lambda_k1[l], lambda_q2[l], lambda_k2[l]]).astype(_F32)
        head = pl.BlockSpec((None, seq, V_HEAD_DIM), lambda b, hd: (b, 0, hd))
        merged = pl.pallas_call(
            functools.partial(_attn_kernel, seq=seq, bk=bk, nq=nq, lambda_init=lambda_init),
            grid=(bsz, N_HEADS),
            in_specs=[_const_spec((4, HEAD_DIM)), head, head,
                      pl.BlockSpec((None, seq // bk, V_HEAD_DIM, bk), lambda b, hd: (b, 0, hd, 0)),
                      head, head, _const_spec((1, V_HEAD_DIM))],
            out_specs=head,
            out_shape=act,
            scratch_shapes=[pltpu.VMEM((2 * nq, V_HEAD_DIM, bk), _BF16),
                            *[pltpu.VMEM((2 * nq, bk, bk), _F32)] * 4, *[pltpu.VMEM((2 * nq, 1, bk), _F32)] * 4,
                            pltpu.VMEM((2 * nq, V_HEAD_DIM + SUM_ROWS, bk), _F32), pltpu.VMEM((2 * nq, 1, bk), _F32)],
            compiler_params=params, name="diff_attn",
        )(lamv, q, k, vt, ga, gc, subln_g[l][None, :])

        tok3 = pl.BlockSpec((None, n_sub3 * tm3, d), lambda b, i: (b, i, 0))
        x = pl.pallas_call(
            functools.partial(_ffn_kernel, tm=tm3, n_sub=n_sub3, cf=cf, n_chunks=n_chunks),
            grid=(bsz, seq // (n_sub3 * tm3)),
            in_specs=[tok3, tok3, _const_spec((d, d)), _const_spec((1, d)), _const_spec((d, 2 * d_ff)),
                      _const_spec((CONV_K, 2 * d_ff)), _const_spec((1, 2 * d_ff)), _const_spec((d_ff, d))],
            out_specs=tok3,
            out_shape=jax.ShapeDtypeStruct((bsz, seq, d), x.dtype),
            scratch_shapes=[pltpu.VMEM((tm3, d), _BF16), pltpu.VMEM((n_chunks, tm3 + HALO, 2 * cf), _F32)],
            compiler_params=params, name="out_proj_ffn",
        )(x, merged, w_out[l].astype(_BF16), ffn_norm_g[l][None, :], w_up[l].astype(_BF16), ffn_conv_w[l],
          ffn_conv_b[l][None, :], w_down[l].astype(_BF16))
    return x
```

```python
import functools
import math

import jax
import jax.numpy as jnp
from jax import lax
from jax.experimental import pallas as pl
from jax.experimental.pallas import tpu as pltpu

N_HEADS = 8
HEAD_DIM = 64
V_HEAD_DIM = 2 * HEAD_DIM
ROPE_THETA = 10000.0
NORM_EPS = 1e-6
CONV_K = 3
HALO = 8
LANES = 128
MXU_TILE = 256
PROJ_ROWS, PROJ_SUBTILES = 512, 2
FFN_ROWS, FFN_SUBTILES = 256, 4
QUERY_PARTS = 4
SUM_ROWS = 16
LEAD = 1
VMEM_LIMIT_BYTES = 56 * 1024 * 1024

_F32 = jnp.float32
_BF16 = jnp.bfloat16


def _sigmoid(x):
    return 1.0 / (1.0 + jnp.exp(-x))


def _proj_kernel(x_ref, g_ref, w_ref, bg_ref, qg_ref, kg_ref, cos_ref, sin_ref, scw_ref, seg_ref,
                 q_ref, k_ref, vt_ref, ga_ref, gc_ref, z_scr, *, tm, n_sub, bk, d):
    cw = MXU_TILE
    n_pieces = d // cw
    seg = seg_ref[...]
    lane = lax.broadcasted_iota(jnp.int32, (tm, LANES), 1)
    first_half = (lane & (HEAD_DIM // 2)) == 0

    @pl.when(pl.program_id(1) == 0)
    def _():
        z_scr[0:HALO, :] = jnp.zeros((HALO, d), _F32)

    def sub_tile(i, carry):
        rows = pl.ds(pl.multiple_of(i * tm, tm), tm)
        x = x_ref[rows, :]
        ms = jnp.mean(x * x, axis=-1, keepdims=True)
        h = (x * lax.rsqrt(ms + NORM_EPS) * g_ref[...]).astype(_BF16)
        cosp = cos_ref[rows, :]
        sinp = sin_ref[rows, :]

        def proj(g, c):
            col = g * d + c * cw
            return jnp.dot(h, w_ref[:, col:col + cw], preferred_element_type=_F32)

        def norm_rope(pc, gam_ref, o_ref, c):
            msq = jnp.dot((pc * pc).astype(_BF16), seg, preferred_element_type=_F32)
            xn = pc * lax.rsqrt(msq + NORM_EPS) * gam_ref[:, c * cw:(c + 1) * cw]
            for hh in range(cw // LANES):
                xh = xn[:, hh * LANES:(hh + 1) * LANES]
                rot = jnp.where(first_half, pltpu.roll(xh, LANES - HEAD_DIM // 2, 1), pltpu.roll(xh, HEAD_DIM // 2, 1))
                col = c * cw + hh * LANES
                o_ref[rows, col:col + LANES] = (xh * cosp + rot * sinp).astype(o_ref.dtype)

        def q_piece(ps, c):
            norm_rope(ps[0], qg_ref, q_ref, c)

        def k_piece(ps, c):
            norm_rope(ps[0], kg_ref, k_ref, c)

        def v_piece(ps, c):
            for r in range(tm // bk):
                vt_ref[i * (tm // bk) + r, c * cw:(c + 1) * cw, :] = ps[0][r * bk:(r + 1) * bk, :].T.astype(vt_ref.dtype)

        def conv_piece(ps, c):
            p_b, p_c, p_x, p_g = ps
            cols = slice(c * cw, (c + 1) * cw)
            z = p_c * p_x
            w = scw_ref[:, cols]
            z_scr[HALO:HALO + tm, cols] = z
            conv = (w[2:3, :] * z + w[1:2, :] * z_scr[HALO - 1:HALO - 1 + tm, cols]
                    + w[0:1, :] * z_scr[HALO - 2:HALO - 2 + tm, cols])
            z_scr[0:HALO, cols] = z_scr[tm:tm + HALO, cols]
            gate = _sigmoid(p_g + bg_ref[:, d + c * cw:d + (c + 1) * cw])
            gc_ref[rows, cols] = (gate * (p_b * conv)).astype(gc_ref.dtype)

        def ga_piece(ps, c):
            cols = slice(c * cw, (c + 1) * cw)
            ga_ref[rows, cols] = _sigmoid(ps[0] + bg_ref[:, cols]).astype(ga_ref.dtype)

        work = [((0,), q_piece), ((1,), k_piece), ((2,), v_piece), ((3, 4, 5, 7), conv_piece), ((6,), ga_piece)]
        units = [(groups, fn, c) for groups, fn in work for c in range(n_pieces)]
        pending = [proj(g, units[0][2]) for g in units[0][0]]
        for u, (groups, fn, c) in enumerate(units):
            ps = pending
            if u + 1 < len(units):
                pending = [proj(g, units[u + 1][2]) for g in units[u + 1][0]]
            fn(ps, c)
        return carry

    lax.fori_loop(0, n_sub, sub_tile, 0)


def _attn_kernel(lam_ref, q_ref, k_ref, vt_ref, ga_ref, gc_ref, sg_ref, o_ref,
                 qz_scr, st_a, st_b, st_c, st_d, mx_a, mx_b, mx_c, mx_d, acc_ref, m_ref, *, seq, bk, nq, lambda_init):
    assert nq == 4, "the diagonal range is absorbed as exactly two pairs of kv blocks"
    bq = nq * bk
    lv = lam_ref[...]
    lam = (jnp.exp(jnp.sum(lv[0:1] * lv[1:2], axis=-1, keepdims=True))
           - jnp.exp(jnp.sum(lv[2:3] * lv[3:4], axis=-1, keepdims=True)) + lambda_init)
    row = lax.broadcasted_iota(jnp.int32, (V_HEAD_DIM, bk), 0)
    diag_ok = lax.broadcasted_iota(jnp.int32, (bk, bk), 0) <= lax.broadcasted_iota(jnp.int32, (bk, bk), 1)
    sg = sg_ref[...]
    ones_rows = jnp.ones((SUM_ROWS, bk), _BF16)
    all_groups = tuple(range(2 * nq))
    pair_ab = ((st_a, mx_a), (st_b, mx_b))
    pair_cd = ((st_c, mx_c), (st_d, mx_d))

    def scores(j, buf, g):
        st, mx = buf
        kb = k_ref[pl.ds(pl.multiple_of(j * bk, bk), bk), :]
        r = jnp.dot(kb, qz_scr[g], preferred_element_type=_F32)
        st[g] = r
        mx[g] = jnp.max(r, axis=0, keepdims=True)

    def absorb(g, blocks):
        tiles, maxes = [], []
        for j, (st, mx), diagonal in blocks:
            s = st[g]
            if diagonal:
                s = jnp.where(diag_ok, s, -jnp.inf)
                maxes.append(jnp.max(s, axis=0, keepdims=True))
            else:
                maxes.append(mx[g])
            tiles.append(s)
        m_old = m_ref[g]
        m_new = functools.reduce(jnp.maximum, maxes, m_old)
        alpha = jnp.exp2(m_old - m_new)
        p = jnp.concatenate([jnp.exp2(s - m_new).astype(_BF16) for s in tiles], axis=0)
        vt = jnp.concatenate([jnp.concatenate([vt_ref[j], ones_rows], axis=0) for j, _, _ in blocks], axis=1)
        m_ref[g] = m_new
        acc_ref[g] = acc_ref[g] * alpha + jnp.dot(vt, p, preferred_element_type=_F32)

    def pair_step(j0, cur, nxt, visible, ahead):
        todo = [g for g in all_groups if ahead(g)]
        work = [g for g in all_groups if visible(g)]

        def issue(gs):
            for g in gs:
                for r in ahead(g):
                    scores(j0 + 2 + r, nxt[r], g)

        issue(todo[:LEAD])
        for i, g in enumerate(work):
            issue(todo[i + LEAD:i + LEAD + 1])
            absorb(g, [(j0 + r, cur[r], diagonal) for r, diagonal in visible(g)])
        issue(todo[len(work) + LEAD:])

    both_visible = lambda g: ((0, False), (1, False))
    both_ahead = lambda g: (0, 1)

    def load_queries(qi):
        q0 = pl.multiple_of(qi * bq, bq)
        qt = q_ref[pl.ds(q0, bq), :].astype(_F32).T
        for g in all_groups:
            qh = qt[:, (g % nq) * bk:(g % nq + 1) * bk]
            keep = (row < HEAD_DIM) if g // nq == 0 else (row >= HEAD_DIM)
            qz_scr[g] = jnp.where(keep, qh, 0.0).astype(_BF16)
        for g in all_groups:
            scores(0, pair_ab[0], g)
            scores(1, pair_ab[1], g)

    def reset_state():
        acc_ref[...] = jnp.zeros_like(acc_ref)
        m_ref[...] = jnp.full_like(m_ref, -jnp.inf)

    def visible_blocks(j0):
        pair_step(j0, pair_ab, pair_cd, both_visible, both_ahead)
        pair_step(j0 + 2, pair_cd, pair_ab, both_visible, both_ahead)

    def eight_blocks(t, c):
        visible_blocks(8 * t)
        visible_blocks(8 * t + 4)
        return c

    def four_blocks(j0):
        def body(t, c):
            visible_blocks(j0)
            return c
        return body

    def normalized(g):
        acc = acc_ref[g]
        return acc[:V_HEAD_DIM] * (1.0 / acc[V_HEAD_DIM:V_HEAD_DIM + 1])

    n_qblocks = seq // bq

    def diagonal_view(first):
        def view(g):
            c = g % nq
            return tuple((r - first, r == c) for r in (first, first + 1) if r <= c)
        return view

    def q_block(qi, carry):
        lax.fori_loop(0, qi // 2, eight_blocks, 0)
        lax.fori_loop(0, qi % 2, four_blocks(8 * (qi // 2)), 0)
        jd = nq * qi
        seen_ab, seen_cd = diagonal_view(0), diagonal_view(2)
        pair_step(jd, pair_ab, pair_cd, seen_ab, lambda g: tuple(r for r, _ in seen_cd(g)))
        pair_step(jd + 2, pair_cd, pair_ab, seen_cd, lambda g: ())

        load_queries(jnp.minimum(qi + 1, n_qblocks - 1))

        q0 = pl.multiple_of(qi * bq, bq)
        for c in range(nq):
            o = normalized(c) - lam * normalized(nq + c)
            on = o * lax.rsqrt(jnp.mean(o * o, axis=0, keepdims=True) + NORM_EPS)
            y = on.T * sg * (1.0 - lambda_init)
            rows = pl.ds(q0 + c * bk, bk)
            o_ref[rows, :] = (ga_ref[rows, :].astype(_F32) * y + gc_ref[rows, :].astype(_F32)).astype(o_ref.dtype)
        reset_state()
        return carry

    load_queries(0)
    reset_state()
    lax.fori_loop(0, n_qblocks, q_block, 0)


def _ffn_kernel(x_ref, mg_ref, wo_ref, g_ref, wu_ref, cw_ref, cb_ref, wd_ref, o_ref,
                h_scr, u_scr, *, tm, n_sub, cf, n_chunks):
    d_ff = cf * n_chunks

    @pl.when(pl.program_id(1) == 0)
    def _():
        for c in range(n_chunks):
            u_scr[c, 0:HALO, :] = jnp.zeros((HALO, 2 * cf), _F32)

    def ab_cols(ref, c):
        return jnp.concatenate([ref[:, c * cf:(c + 1) * cf], ref[:, d_ff + c * cf:d_ff + (c + 1) * cf]], axis=1)

    def up(c):
        return jnp.dot(h_scr[...], ab_cols(wu_ref, c), preferred_element_type=_F32)

    def sub_tile(i, carry):
        rows = pl.ds(pl.multiple_of(i * tm, tm), tm)
        x1 = x_ref[rows, :] + jnp.dot(mg_ref[rows, :], wo_ref[...], preferred_element_type=_F32)
        ms = jnp.mean(x1 * x1, axis=-1, keepdims=True)
        h_scr[...] = (x1 * lax.rsqrt(ms + NORM_EPS) * g_ref[...]).astype(_BF16)

        acc = x1
        u_next = up(0)
        for c in range(n_chunks):
            u = u_next
            if c + 1 < n_chunks:
                u_next = up(c + 1)
            u_scr[c, HALO:HALO + tm, :] = u
            cw = ab_cols(cw_ref, c)
            uc = (cw[2:3, :] * u + cw[1:2, :] * u_scr[c, HALO - 1:HALO - 1 + tm, :]
                  + cw[0:1, :] * u_scr[c, HALO - 2:HALO - 2 + tm, :] + ab_cols(cb_ref, c))
            u_scr[c, 0:HALO, :] = u_scr[c, tm:tm + HALO, :]
            a = uc[:, :cf]
            act = (a * _sigmoid(a) * uc[:, cf:]).astype(_BF16)
            acc = acc + jnp.dot(act, wd_ref[c * cf:(c + 1) * cf, :], preferred_element_type=_F32)
        o_ref[rows, :] = acc
        return carry

    lax.fori_loop(0, n_sub, sub_tile, 0)


def _const_spec(shape):
    nd = len(shape)
    return pl.BlockSpec(shape, lambda *_: (0,) * nd, pipeline_mode=pl.Buffered(1))


def _rope_tables(seq):
    inv = ROPE_THETA ** (-jnp.arange(0, HEAD_DIM, 2, dtype=_F32) / HEAD_DIM)
    ang = jnp.arange(seq, dtype=_F32)[:, None] * inv[None, :]
    c, s = jnp.cos(ang), jnp.sin(ang)
    return jnp.concatenate([c, c, c, c], axis=1), jnp.concatenate([-s, s, -s, s], axis=1)


def kernel(x, attn_norm_g, w_in, b_gate, q_norm_g, k_norm_g, lambda_q1, lambda_k1, lambda_q2, lambda_k2,
           subln_g, short_conv_w, w_out, ffn_norm_g, w_up, ffn_conv_w, ffn_conv_b, w_down):
    bsz, seq, d = x.shape
    depth = attn_norm_g.shape[0]
    d_ff = w_down.shape[1]
    assert d == N_HEADS * V_HEAD_DIM and w_in.shape[2] == 8 * d

    tm1 = min(PROJ_ROWS, seq)
    n_sub1 = PROJ_SUBTILES if seq % (PROJ_SUBTILES * tm1) == 0 else 1
    bk = MXU_TILE
    nq = QUERY_PARTS
    bq = nq * bk
    tm3 = min(FFN_ROWS, seq)
    n_sub3 = FFN_SUBTILES if seq % (FFN_SUBTILES * tm3) == 0 else 1
    cf = MXU_TILE
    n_chunks = d_ff // cf
    assert seq % tm1 == 0 and seq % bq == 0 and seq % tm3 == 0 and d_ff % cf == 0 and tm1 % bk == 0

    cosp, sinp = _rope_tables(seq)
    seg = jnp.kron(jnp.eye(MXU_TILE // HEAD_DIM, dtype=_F32), jnp.full((HEAD_DIM, HEAD_DIM), 1.0 / HEAD_DIM, _F32)).astype(_BF16)
    params = pltpu.CompilerParams(dimension_semantics=("arbitrary", "arbitrary"), vmem_limit_bytes=VMEM_LIMIT_BYTES)

    for l in range(depth):
        lambda_init = 0.8 - 0.6 * math.exp(-0.3 * l)
        qg = jnp.tile(q_norm_g[l] * (HEAD_DIM ** -0.5 * math.log2(math.e)), 2 * N_HEADS)[None, :]
        kg = jnp.tile(k_norm_g[l], 2 * N_HEADS)[None, :]

        act = jax.ShapeDtypeStruct((bsz, seq, d), _BF16)
        blk1 = n_sub1 * tm1
        tok1 = pl.BlockSpec((None, blk1, d), lambda b, i: (b, i, 0))
        q, k, vt, ga, gc = pl.pallas_call(
            functools.partial(_proj_kernel, tm=tm1, n_sub=n_sub1, bk=bk, d=d),
            grid=(bsz, seq // blk1),
            in_specs=[tok1, _const_spec((1, d)), _const_spec((d, 8 * d)), _const_spec((1, 2 * d)),
                      _const_spec((1, d)), _const_spec((1, d)),
                      pl.BlockSpec((blk1, LANES), lambda b, i: (i, 0)), pl.BlockSpec((blk1, LANES), lambda b, i: (i, 0)),
                      _const_spec((CONV_K, d)), _const_spec((MXU_TILE, MXU_TILE))],
            out_specs=[tok1, tok1, pl.BlockSpec((None, blk1 // bk, d, bk), lambda b, i: (b, i, 0, 0)), tok1, tok1],
            out_shape=[act, act, jax.ShapeDtypeStruct((bsz, seq // bk, d, bk), _BF16), act, act],
            scratch_shapes=[pltpu.VMEM((tm1 + HALO, d), _F32)],
            compiler_params=params, name="norm_proj",
        )(x, attn_norm_g[l][None, :], w_in[l].astype(_BF16), b_gate[l][None, :], qg, kg, cosp, sinp,
          short_conv_w[l], seg)

        lamv = jnp.stack([lambda_q1[l], lambda_k1[l], lambda_q2[l], lambda_k2[l]]).astype(_F32)
        head = pl.BlockSpec((None, seq, V_HEAD_DIM), lambda b, hd: (b, 0, hd))
        merged = pl.pallas_call(
            functools.partial(_attn_kernel, seq=seq, bk=bk, nq=nq, lambda_init=lambda_init),
            grid=(bsz, N_HEADS),
            in_specs=[_const_spec((4, HEAD_DIM)), head, head,
                      pl.BlockSpec((None, seq // bk, V_HEAD_DIM, bk), lambda b, hd: (b, 0, hd, 0)),
                      head, head, _const_spec((1, V_HEAD_DIM))],
            out_specs=head,
            out_shape=act,
            scratch_shapes=[pltpu.VMEM((2 * nq, V_HEAD_DIM, bk), _BF16),
                            *[pltpu.VMEM((2 * nq, bk, bk), _F32)] * 4, *[pltpu.VMEM((2 * nq, 1, bk), _F32)] * 4,
                            pltpu.VMEM((2 * nq, V_HEAD_DIM + SUM_ROWS, bk), _F32), pltpu.VMEM((2 * nq, 1, bk), _F32)],
            compiler_params=params, name="diff_attn",
        )(lamv, q, k, vt, ga, gc, subln_g[l][None, :])

        tok3 = pl.BlockSpec((None, n_sub3 * tm3, d), lambda b, i: (b, i, 0))
        x = pl.pallas_call(
            functools.partial(_ffn_kernel, tm=tm3, n_sub=n_sub3, cf=cf, n_chunks=n_chunks),
            grid=(bsz, seq // (n_sub3 * tm3)),
            in_specs=[tok3, tok3, _const_spec((d, d)), _const_spec((1, d)), _const_spec((d, 2 * d_ff)),
                      _const_spec((CONV_K, 2 * d_ff)), _const_spec((1, 2 * d_ff)), _const_spec((d_ff, d))],
            out_specs=tok3,
            out_shape=jax.ShapeDtypeStruct((bsz, seq, d), x.dtype),
            scratch_shapes=[pltpu.VMEM((tm3, d), _BF16), pltpu.VMEM((n_chunks, tm3 + HALO, 2 * cf), _F32)],
            compiler_params=params, name="out_proj_ffn",
        )(x, merged, w_out[l].astype(_BF16), ffn_norm_g[l][None, :], w_up[l].astype(_BF16), ffn_conv_w[l],
          ffn_conv_b[l][None, :], w_down[l].astype(_BF16))
    return x
```

```python
import functools
import math

import jax
import jax.numpy as jnp
from jax import lax
from jax.experimental import pallas as pl
from jax.experimental.pallas import tpu as pltpu

N_HEADS = 8
HEAD_DIM = 64
V_HEAD_DIM = 2 * HEAD_DIM
ROPE_THETA = 10000.0
NORM_EPS = 1e-6
CONV_K = 3
HALO = 8
LANES = 128
MXU_TILE = 256
PROJ_ROWS, PROJ_SUBTILES = 512, 2
FFN_ROWS, FFN_SUBTILES = 256, 4
QUERY_PARTS = 4
SUM_ROWS = 16
LEAD = 1
PARAM_ROWS = 16
VMEM_LIMIT_BYTES = 56 * 1024 * 1024

_F32 = jnp.float32
_BF16 = jnp.bfloat16


def _sigmoid(x):
    return 1.0 / (1.0 + jnp.exp(-x))


def _proj_kernel(x_ref, g_ref, w_ref, bg_ref, qg_ref, kg_ref, cos_ref, sin_ref, scw_ref, seg_ref,
                 q_ref, k_ref, vt_ref, ga_ref, gc_ref, z_scr, *, tm, n_sub, bk, d):
    cw = MXU_TILE
    n_pieces = d // cw
    seg = seg_ref[...]
    lane = lax.broadcasted_iota(jnp.int32, (tm, LANES), 1)
    first_half = (lane & (HEAD_DIM // 2)) == 0

    @pl.when(pl.program_id(1) == 0)
    def _():
        z_scr[0:HALO, :] = jnp.zeros((HALO, d), _F32)

    def sub_tile(i, carry):
        rows = pl.ds(pl.multiple_of(i * tm, tm), tm)
        x = x_ref[rows, :]
        ms = jnp.mean(x * x, axis=-1, keepdims=True)
        h = (x * lax.rsqrt(ms + NORM_EPS) * g_ref[...]).astype(_BF16)
        cosp = cos_ref[rows, :]
        sinp = sin_ref[rows, :]

        def proj(g, c):
            col = g * d + c * cw
            return jnp.dot(h, w_ref[:, col:col + cw], preferred_element_type=_F32)

        def norm_rope(pc, gam_ref, o_ref, c):
            msq = jnp.dot((pc * pc).astype(_BF16), seg, preferred_element_type=_F32)
            xn = pc * lax.rsqrt(msq + NORM_EPS) * gam_ref[:, c * cw:(c + 1) * cw]
            for hh in range(cw // LANES):
                xh = xn[:, hh * LANES:(hh + 1) * LANES]
                rot = jnp.where(first_half, pltpu.roll(xh, LANES - HEAD_DIM // 2, 1), pltpu.roll(xh, HEAD_DIM // 2, 1))
                col = c * cw + hh * LANES
                o_ref[rows, col:col + LANES] = (xh * cosp + rot * sinp).astype(o_ref.dtype)

        def q_piece(ps, c):
            norm_rope(ps[0], qg_ref, q_ref, c)

        def k_piece(ps, c):
            norm_rope(ps[0], kg_ref, k_ref, c)

        def v_piece(ps, c):
            for r in range(tm // bk):
                vt_ref[i * (tm // bk) + r, c * cw:(c + 1) * cw, :] = ps[0][r * bk:(r + 1) * bk, :].T.astype(vt_ref.dtype)

        def conv_piece(ps, c):
            p_b, p_c, p_x, p_g = ps
            cols = slice(c * cw, (c + 1) * cw)
            z = p_c * p_x
            w = scw_ref[:, cols]
            z_scr[HALO:HALO + tm, cols] = z
            conv = (w[2:3, :] * z + w[1:2, :] * z_scr[HALO - 1:HALO - 1 + tm, cols]
                    + w[0:1, :] * z_scr[HALO - 2:HALO - 2 + tm, cols])
            z_scr[0:HALO, cols] = z_scr[tm:tm + HALO, cols]
            gate = _sigmoid(p_g + bg_ref[:, d + c * cw:d + (c + 1) * cw])
            gc_ref[rows, cols] = (gate * (p_b * conv)).astype(gc_ref.dtype)

        def ga_piece(ps, c):
            cols = slice(c * cw, (c + 1) * cw)
            ga_ref[rows, cols] = _sigmoid(ps[0] + bg_ref[:, cols]).astype(ga_ref.dtype)

        work = [((0,), q_piece), ((1,), k_piece), ((2,), v_piece), ((3, 4, 5, 7), conv_piece), ((6,), ga_piece)]
        units = [(groups, fn, c) for groups, fn in work for c in range(n_pieces)]
        pending = [proj(g, units[0][2]) for g in units[0][0]]
        for u, (groups, fn, c) in enumerate(units):
            ps = pending
            if u + 1 < len(units):
                pending = [proj(g, units[u + 1][2]) for g in units[u + 1][0]]
            fn(ps, c)
        return carry

    lax.fori_loop(0, n_sub, sub_tile, 0)


def _attn_kernel(par_ref, q_ref, k_ref, vt_ref, ga_ref, gc_ref, o_ref,
                 qz_scr, st_a, st_b, st_c, st_d, mx_a, mx_b, mx_c, mx_d, acc_ref, m_ref, *, seq, bk, nq, lambda_init):
    assert nq == 4, "the diagonal range is absorbed as exactly two pairs of kv blocks"
    bq = nq * bk
    lv = par_ref[0:4, 0:HEAD_DIM]
    lam = (jnp.exp(jnp.sum(lv[0:1] * lv[1:2], axis=-1, keepdims=True))
           - jnp.exp(jnp.sum(lv[2:3] * lv[3:4], axis=-1, keepdims=True)) + lambda_init)
    row = lax.broadcasted_iota(jnp.int32, (V_HEAD_DIM, bk), 0)
    diag_ok = lax.broadcasted_iota(jnp.int32, (bk, bk), 0) <= lax.broadcasted_iota(jnp.int32, (bk, bk), 1)
    sg = par_ref[PARAM_ROWS // 2:PARAM_ROWS // 2 + 1, :]
    ones_rows = jnp.ones((SUM_ROWS, bk), _BF16)
    all_groups = tuple(range(2 * nq))
    pair_ab = ((st_a, mx_a), (st_b, mx_b))
    pair_cd = ((st_c, mx_c), (st_d, mx_d))

    def scores(j, buf, g):
        st, mx = buf
        kb = k_ref[pl.ds(pl.multiple_of(j * bk, bk), bk), :]
        r = jnp.dot(kb, qz_scr[g], preferred_element_type=_F32)
        st[g] = r
        mx[g] = jnp.max(r, axis=0, keepdims=True)

    def absorb(g, blocks):
        tiles, maxes = [], []
        for j, (st, mx), diagonal in blocks:
            s = st[g]
            if diagonal:
                s = jnp.where(diag_ok, s, -jnp.inf)
                maxes.append(jnp.max(s, axis=0, keepdims=True))
            else:
                maxes.append(mx[g])
            tiles.append(s)
        m_old = m_ref[g]
        m_new = functools.reduce(jnp.maximum, maxes, m_old)
        alpha = jnp.exp2(m_old - m_new)
        p = jnp.concatenate([jnp.exp2(s - m_new).astype(_BF16) for s in tiles], axis=0)
        vt = jnp.concatenate([jnp.concatenate([vt_ref[j], ones_rows], axis=0) for j, _, _ in blocks], axis=1)
        m_ref[g] = m_new
        acc_ref[g] = acc_ref[g] * alpha + jnp.dot(vt, p, preferred_element_type=_F32)

    def pair_step(j0, cur, nxt, visible, ahead):
        todo = [g for g in all_groups if ahead(g)]
        work = [g for g in all_groups if visible(g)]

        def issue(gs):
            for g in gs:
                for r in ahead(g):
                    scores(j0 + 2 + r, nxt[r], g)

        issue(todo[:LEAD])
        for i, g in enumerate(work):
            issue(todo[i + LEAD:i + LEAD + 1])
            absorb(g, [(j0 + r, cur[r], diagonal) for r, diagonal in visible(g)])
        issue(todo[len(work) + LEAD:])

    both_visible = lambda g: ((0, False), (1, False))
    both_ahead = lambda g: (0, 1)

    def load_queries(qi):
        q0 = pl.multiple_of(qi * bq, bq)
        qt = q_ref[pl.ds(q0, bq), :].astype(_F32).T
        for g in all_groups:
            qh = qt[:, (g % nq) * bk:(g % nq + 1) * bk]
            keep = (row < HEAD_DIM) if g // nq == 0 else (row >= HEAD_DIM)
            qz_scr[g] = jnp.where(keep, qh, 0.0).astype(_BF16)
        for g in all_groups:
            scores(0, pair_ab[0], g)
            scores(1, pair_ab[1], g)

    def reset_state():
        acc_ref[...] = jnp.zeros_like(acc_ref)
        m_ref[...] = jnp.full_like(m_ref, -jnp.inf)

    def visible_blocks(j0):
        pair_step(j0, pair_ab, pair_cd, both_visible, both_ahead)
        pair_step(j0 + 2, pair_cd, pair_ab, both_visible, both_ahead)

    def eight_blocks(t, c):
        visible_blocks(8 * t)
        visible_blocks(8 * t + 4)
        return c

    def four_blocks(j0):
        def body(t, c):
            visible_blocks(j0)
            return c
        return body

    def normalized(g):
        acc = acc_ref[g]
        return acc[:V_HEAD_DIM] * (1.0 / acc[V_HEAD_DIM:V_HEAD_DIM + 1])

    n_qblocks = seq // bq

    def diagonal_view(first):
        def view(g):
            c = g % nq
            return tuple((r - first, r == c) for r in (first, first + 1) if r <= c)
        return view

    def q_block(qi, carry):
        lax.fori_loop(0, qi // 2, eight_blocks, 0)
        lax.fori_loop(0, qi % 2, four_blocks(8 * (qi // 2)), 0)
        jd = nq * qi
        seen_ab, seen_cd = diagonal_view(0), diagonal_view(2)
        pair_step(jd, pair_ab, pair_cd, seen_ab, lambda g: tuple(r for r, _ in seen_cd(g)))
        pair_step(jd + 2, pair_cd, pair_ab, seen_cd, lambda g: ())

        load_queries(jnp.minimum(qi + 1, n_qblocks - 1))

        q0 = pl.multiple_of(qi * bq, bq)
        for c in range(nq):
            o = normalized(c) - lam * normalized(nq + c)
            on = o * lax.rsqrt(jnp.mean(o * o, axis=0, keepdims=True) + NORM_EPS)
            y = on.T * sg * (1.0 - lambda_init)
            rows = pl.ds(q0 + c * bk, bk)
            o_ref[rows, :] = (ga_ref[rows, :].astype(_F32) * y + gc_ref[rows, :].astype(_F32)).astype(o_ref.dtype)
        reset_state()
        return carry

    load_queries(0)
    reset_state()
    lax.fori_loop(0, n_qblocks, q_block, 0)


def _ffn_kernel(x_ref, mg_ref, wo_ref, g_ref, wu_ref, cw_ref, cb_ref, wd_ref, o_ref,
                h_scr, u_scr, *, tm, n_sub, cf, n_chunks):
    d_ff = cf * n_chunks

    @pl.when(pl.program_id(1) == 0)
    def _():
        for c in range(n_chunks):
            u_scr[c, 0:HALO, :] = jnp.zeros((HALO, 2 * cf), _F32)

    def ab_cols(ref, c):
        return jnp.concatenate([ref[:, c * cf:(c + 1) * cf], ref[:, d_ff + c * cf:d_ff + (c + 1) * cf]], axis=1)

    def up(c):
        return jnp.dot(h_scr[...], ab_cols(wu_ref, c), preferred_element_type=_F32)

    def sub_tile(i, carry):
        rows = pl.ds(pl.multiple_of(i * tm, tm), tm)
        x1 = x_ref[rows, :] + jnp.dot(mg_ref[rows, :], wo_ref[...], preferred_element_type=_F32)
        ms = jnp.mean(x1 * x1, axis=-1, keepdims=True)
        h_scr[...] = (x1 * lax.rsqrt(ms + NORM_EPS) * g_ref[...]).astype(_BF16)

        acc = x1
        u_next = up(0)
        for c in range(n_chunks):
            u = u_next
            if c + 1 < n_chunks:
                u_next = up(c + 1)
            u_scr[c, HALO:HALO + tm, :] = u
            cw = ab_cols(cw_ref, c)
            uc = (cw[2:3, :] * u + cw[1:2, :] * u_scr[c, HALO - 1:HALO - 1 + tm, :]
                  + cw[0:1, :] * u_scr[c, HALO - 2:HALO - 2 + tm, :] + ab_cols(cb_ref, c))
            u_scr[c, 0:HALO, :] = u_scr[c, tm:tm + HALO, :]
            a = uc[:, :cf]
            act = (a * _sigmoid(a) * uc[:, cf:]).astype(_BF16)
            acc = acc + jnp.dot(act, wd_ref[c * cf:(c + 1) * cf, :], preferred_element_type=_F32)
        o_ref[rows, :] = acc
        return carry

    lax.fori_loop(0, n_sub, sub_tile, 0)


def _const_spec(shape):
    nd = len(shape)
    return pl.BlockSpec(shape, lambda *_: (0,) * nd, pipeline_mode=pl.Buffered(1))


def _rope_tables(seq):
    inv = ROPE_THETA ** (-jnp.arange(0, HEAD_DIM, 2, dtype=_F32) / HEAD_DIM)
    ang = jnp.arange(seq, dtype=_F32)[:, None] * inv[None, :]
    c, s = jnp.cos(ang), jnp.sin(ang)
    return jnp.concatenate([c, c, c, c], axis=1), jnp.concatenate([-s, s, -s, s], axis=1)


def kernel(x, attn_norm_g, w_in, b_gate, q_norm_g, k_norm_g, lambda_q1, lambda_k1, lambda_q2, lambda_k2,
           subln_g, short_conv_w, w_out, ffn_norm_g, w_up, ffn_conv_w, ffn_conv_b, w_down):
    bsz, seq, d = x.shape
    depth = attn_norm_g.shape[0]
    d_ff = w_down.shape[1]
    assert d == N_HEADS * V_HEAD_DIM and w_in.shape[2] == 8 * d

    tm1 = min(PROJ_ROWS, seq)
    n_sub1 = PROJ_SUBTILES if seq % (PROJ_SUBTILES * tm1) == 0 else 1
    bk = MXU_TILE
    nq = QUERY_PARTS
    bq = nq * bk
    tm3 = min(FFN_ROWS, seq)
    n_sub3 = FFN_SUBTILES if seq % (FFN_SUBTILES * tm3) == 0 else 1
    cf = MXU_TILE
    n_chunks = d_ff // cf
    assert seq % tm1 == 0 and seq % bq == 0 and seq % tm3 == 0 and d_ff % cf == 0 and tm1 % bk == 0

    cosp, sinp = _rope_tables(seq)
    seg = jnp.kron(jnp.eye(MXU_TILE // HEAD_DIM, dtype=_F32), jnp.full((HEAD_DIM, HEAD_DIM), 1.0 / HEAD_DIM, _F32)).astype(_BF16)
    params = pltpu.CompilerParams(dimension_semantics=("arbitrary", "arbitrary"), vmem_limit_bytes=VMEM_LIMIT_BYTES)

    for l in range(depth):
        lambda_init = 0.8 - 0.6 * math.exp(-0.3 * l)
        qg = jnp.tile(q_norm_g[l] * (HEAD_DIM ** -0.5 * math.log2(math.e)), 2 * N_HEADS)[None, :]
        kg = jnp.tile(k_norm_g[l], 2 * N_HEADS)[None, :]

        act = jax.ShapeDtypeStruct((bsz, seq, d), _BF16)
        blk1 = n_sub1 * tm1
        tok1 = pl.BlockSpec((None, blk1, d), lambda b, i: (b, i, 0))
        q, k, vt, ga, gc = pl.pallas_call(
            functools.partial(_proj_kernel, tm=tm1, n_sub=n_sub1, bk=bk, d=d),
            grid=(bsz, seq // blk1),
            in_specs=[tok1, _const_spec((1, d)), _const_spec((d, 8 * d)), _const_spec((1, 2 * d)),
                      _const_spec((1, d)), _const_spec((1, d)),
                      pl.BlockSpec((blk1, LANES), lambda b, i: (i, 0)), pl.BlockSpec((blk1, LANES), lambda b, i: (i, 0)),
                      _const_spec((CONV_K, d)), _const_spec((MXU_TILE, MXU_TILE))],
            out_specs=[tok1, tok1, pl.BlockSpec((None, blk1 // bk, d, bk), lambda b, i: (b, i, 0, 0)), tok1, tok1],
            out_shape=[act, act, jax.ShapeDtypeStruct((bsz, seq // bk, d, bk), _BF16), act, act],
            scratch_shapes=[pltpu.VMEM((tm1 + HALO, d), _F32)],
            compiler_params=params, name="norm_proj",
        )(x, attn_norm_g[l][None, :], w_in[l].astype(_BF16), b_gate[l][None, :], qg, kg, cosp, sinp,
          short_conv_w[l], seg)

        lam4 = jnp.stack([lambda_q1[l], lambda_k1[l], lambda_q2[l], lambda_k2[l]]).astype(_F32)
        attn_par = jnp.zeros((PARAM_ROWS, LANES), _F32).at[0:4, 0:HEAD_DIM].set(lam4).at[PARAM_ROWS // 2, :].set(subln_g[l])
        head = pl.BlockSpec((None, seq, V_HEAD_DIM), lambda b, hd: (b, 0, hd))
        merged = pl.pallas_call(
            functools.partial(_attn_kernel, seq=seq, bk=bk, nq=nq, lambda_init=lambda_init),
            grid=(bsz, N_HEADS),
            in_specs=[_const_spec((PARAM_ROWS, LANES)), head, head,
                      pl.BlockSpec((None, seq // bk, V_HEAD_DIM, bk), lambda b, hd: (b, 0, hd, 0)),
                      head, head],
            out_specs=head,
            out_shape=act,
            scratch_shapes=[pltpu.VMEM((2 * nq, V_HEAD_DIM, bk), _BF16),
                            *[pltpu.VMEM((2 * nq, bk, bk), _F32)] * 4, *[pltpu.VMEM((2 * nq, 1, bk), _F32)] * 4,
                            pltpu.VMEM((2 * nq, V_HEAD_DIM + SUM_ROWS, bk), _F32), pltpu.VMEM((2 * nq, 1, bk), _F32)],
            compiler_params=params, name="diff_attn",
        )(attn_par, q, k, vt, ga, gc)

        tok3 = pl.BlockSpec((None, n_sub3 * tm3, d), lambda b, i: (b, i, 0))
        x = pl.pallas_call(
            functools.partial(_ffn_kernel, tm=tm3, n_sub=n_sub3, cf=cf, n_chunks=n_chunks),
            grid=(bsz, seq // (n_sub3 * tm3)),
            in_specs=[tok3, tok3, _const_spec((d, d)), _const_spec((1, d)), _const_spec((d, 2 * d_ff)),
                      _const_spec((CONV_K, 2 * d_ff)), _const_spec((1, 2 * d_ff)), _const_spec((d_ff, d))],
            out_specs=tok3,
            out_shape=jax.ShapeDtypeStruct((bsz, seq, d), x.dtype),
            scratch_shapes=[pltpu.VMEM((tm3, d), _BF16), pltpu.VMEM((n_chunks, tm3 + HALO, 2 * cf), _F32)],
            compiler_params=params, name="out_proj_ffn",
        )(x, merged, w_out[l].astype(_BF16), ffn_norm_g[l][None, :], w_up[l].astype(_BF16), ffn_conv_w[l],
          ffn_conv_b[l][None, :], w_down[l].astype(_BF16))
    return x
```

```python
import functools
import math

import jax
import jax.numpy as jnp
from jax import lax
from jax.experimental import pallas as pl
from jax.experimental.pallas import tpu as pltpu

N_HEADS = 8
HEAD_DIM = 64
V_HEAD_DIM = 2 * HEAD_DIM
ROPE_THETA = 10000.0
NORM_EPS = 1e-6
CONV_K = 3
HALO = 8
LANES = 128
MXU_TILE = 256
PROJ_ROWS, PROJ_SUBTILES = 512, 2
FFN_ROWS, FFN_SUBTILES = 256, 4
QUERY_PARTS = 4
SUM_ROWS = 16
LEAD = 1
PARAM_ROWS = 16
VMEM_LIMIT_BYTES = 56 * 1024 * 1024

_F32 = jnp.float32
_BF16 = jnp.bfloat16


def _sigmoid(x):
    return 1.0 / (1.0 + jnp.exp(-x))


def _proj_kernel(x_ref, w_ref, cos_ref, sin_ref, seg_ref, scw_ref, par_ref,
                 q_ref, k_ref, vt_ref, ga_ref, gc_ref, z_scr, *, tm, n_sub, bk, d):
    cw = MXU_TILE
    n_pieces = d // cw
    seg = seg_ref[...]
    lane = lax.broadcasted_iota(jnp.int32, (tm, LANES), 1)
    first_half = (lane & (HEAD_DIM // 2)) == 0

    @pl.when(pl.program_id(1) == 0)
    def _():
        z_scr[0:HALO, :] = jnp.zeros((HALO, d), _F32)

    def sub_tile(i, carry):
        rows = pl.ds(pl.multiple_of(i * tm, tm), tm)
        x = x_ref[rows, :]
        ms = jnp.mean(x * x, axis=-1, keepdims=True)
        h = (x * lax.rsqrt(ms + NORM_EPS) * par_ref[0:1, :]).astype(_BF16)
        cosp = cos_ref[rows, :]
        sinp = sin_ref[rows, :]

        def proj(g, c):
            col = g * d + c * cw
            return jnp.dot(h, w_ref[:, col:col + cw], preferred_element_type=_F32)

        def norm_rope(pc, gam_row, o_ref, c):
            msq = jnp.dot((pc * pc).astype(_BF16), seg, preferred_element_type=_F32)
            xn = pc * lax.rsqrt(msq + NORM_EPS) * par_ref[gam_row:gam_row + 1, c * cw:(c + 1) * cw]
            for hh in range(cw // LANES):
                xh = xn[:, hh * LANES:(hh + 1) * LANES]
                rot = jnp.where(first_half, pltpu.roll(xh, LANES - HEAD_DIM // 2, 1), pltpu.roll(xh, HEAD_DIM // 2, 1))
                col = c * cw + hh * LANES
                o_ref[rows, col:col + LANES] = (xh * cosp + rot * sinp).astype(o_ref.dtype)

        def q_piece(ps, c):
            norm_rope(ps[0], 1, q_ref, c)

        def k_piece(ps, c):
            norm_rope(ps[0], 2, k_ref, c)

        def v_piece(ps, c):
            for r in range(tm // bk):
                vt_ref[i * (tm // bk) + r, c * cw:(c + 1) * cw, :] = ps[0][r * bk:(r + 1) * bk, :].T.astype(vt_ref.dtype)

        def conv_piece(ps, c):
            p_b, p_c, p_x, p_g = ps
            cols = slice(c * cw, (c + 1) * cw)
            z = p_c * p_x
            w = scw_ref[:, cols]
            z_scr[HALO:HALO + tm, cols] = z
            conv = (w[2:3, :] * z + w[1:2, :] * z_scr[HALO - 1:HALO - 1 + tm, cols]
                    + w[0:1, :] * z_scr[HALO - 2:HALO - 2 + tm, cols])
            z_scr[0:HALO, cols] = z_scr[tm:tm + HALO, cols]
            gate = _sigmoid(p_g + par_ref[4:5, cols])
            gc_ref[rows, cols] = (gate * (p_b * conv)).astype(gc_ref.dtype)

        def ga_piece(ps, c):
            cols = slice(c * cw, (c + 1) * cw)
            ga_ref[rows, cols] = _sigmoid(ps[0] + par_ref[3:4, cols]).astype(ga_ref.dtype)

        work = [((0,), q_piece), ((1,), k_piece), ((2,), v_piece), ((3, 4, 5, 7), conv_piece), ((6,), ga_piece)]
        units = [(groups, fn, c) for groups, fn in work for c in range(n_pieces)]
        pending = [proj(g, units[0][2]) for g in units[0][0]]
        for u, (groups, fn, c) in enumerate(units):
            ps = pending
            if u + 1 < len(units):
                pending = [proj(g, units[u + 1][2]) for g in units[u + 1][0]]
            fn(ps, c)
        return carry

    lax.fori_loop(0, n_sub, sub_tile, 0)


def _attn_kernel(par_ref, q_ref, k_ref, vt_ref, ga_ref, gc_ref, o_ref,
                 qz_scr, st_a, st_b, st_c, st_d, mx_a, mx_b, mx_c, mx_d, acc_ref, m_ref, *, seq, bk, nq, lambda_init):
    assert nq == 4, "the diagonal range is absorbed as exactly two pairs of kv blocks"
    bq = nq * bk
    lv = par_ref[0:4, 0:HEAD_DIM]
    lam = (jnp.exp(jnp.sum(lv[0:1] * lv[1:2], axis=-1, keepdims=True))
           - jnp.exp(jnp.sum(lv[2:3] * lv[3:4], axis=-1, keepdims=True)) + lambda_init)
    row = lax.broadcasted_iota(jnp.int32, (V_HEAD_DIM, bk), 0)
    diag_ok = lax.broadcasted_iota(jnp.int32, (bk, bk), 0) <= lax.broadcasted_iota(jnp.int32, (bk, bk), 1)
    sg = par_ref[PARAM_ROWS // 2:PARAM_ROWS // 2 + 1, :]
    ones_rows = jnp.ones((SUM_ROWS, bk), _BF16)
    all_groups = tuple(range(2 * nq))
    pair_ab = ((st_a, mx_a), (st_b, mx_b))
    pair_cd = ((st_c, mx_c), (st_d, mx_d))

    def scores(j, buf, g):
        st, mx = buf
        kb = k_ref[pl.ds(pl.multiple_of(j * bk, bk), bk), :]
        r = jnp.dot(kb, qz_scr[g], preferred_element_type=_F32)
        st[g] = r
        mx[g] = jnp.max(r, axis=0, keepdims=True)

    def absorb(g, blocks):
        tiles, maxes = [], []
        for j, (st, mx), diagonal in blocks:
            s = st[g]
            if diagonal:
                s = jnp.where(diag_ok, s, -jnp.inf)
                maxes.append(jnp.max(s, axis=0, keepdims=True))
            else:
                maxes.append(mx[g])
            tiles.append(s)
        m_old = m_ref[g]
        m_new = functools.reduce(jnp.maximum, maxes, m_old)
        alpha = jnp.exp2(m_old - m_new)
        p = jnp.concatenate([jnp.exp2(s - m_new).astype(_BF16) for s in tiles], axis=0)
        vt = jnp.concatenate([jnp.concatenate([vt_ref[j], ones_rows], axis=0) for j, _, _ in blocks], axis=1)
        m_ref[g] = m_new
        acc_ref[g] = acc_ref[g] * alpha + jnp.dot(vt, p, preferred_element_type=_F32)

    def pair_step(j0, cur, nxt, visible, ahead):
        todo = [g for g in all_groups if ahead(g)]
        work = [g for g in all_groups if visible(g)]

        def issue(gs):
            for g in gs:
                for r in ahead(g):
                    scores(j0 + 2 + r, nxt[r], g)

        issue(todo[:LEAD])
        for i, g in enumerate(work):
            issue(todo[i + LEAD:i + LEAD + 1])
            absorb(g, [(j0 + r, cur[r], diagonal) for r, diagonal in visible(g)])
        issue(todo[len(work) + LEAD:])

    both_visible = lambda g: ((0, False), (1, False))
    both_ahead = lambda g: (0, 1)

    def load_queries(qi):
        q0 = pl.multiple_of(qi * bq, bq)
        qt = q_ref[pl.ds(q0, bq), :].astype(_F32).T
        for g in all_groups:
            qh = qt[:, (g % nq) * bk:(g % nq + 1) * bk]
            keep = (row < HEAD_DIM) if g // nq == 0 else (row >= HEAD_DIM)
            qz_scr[g] = jnp.where(keep, qh, 0.0).astype(_BF16)
        for g in all_groups:
            scores(0, pair_ab[0], g)
            scores(1, pair_ab[1], g)

    def reset_state():
        acc_ref[...] = jnp.zeros_like(acc_ref)
        m_ref[...] = jnp.full_like(m_ref, -jnp.inf)

    def visible_blocks(j0):
        pair_step(j0, pair_ab, pair_cd, both_visible, both_ahead)
        pair_step(j0 + 2, pair_cd, pair_ab, both_visible, both_ahead)

    def eight_blocks(t, c):
        visible_blocks(8 * t)
        visible_blocks(8 * t + 4)
        return c

    def four_blocks(j0):
        def body(t, c):
            visible_blocks(j0)
            return c
        return body

    def normalized(g):
        acc = acc_ref[g]
        return acc[:V_HEAD_DIM] * (1.0 / acc[V_HEAD_DIM:V_HEAD_DIM + 1])

    n_qblocks = seq // bq

    def diagonal_view(first):
        def view(g):
            c = g % nq
            return tuple((r - first, r == c) for r in (first, first + 1) if r <= c)
        return view

    def q_block(qi, carry):
        lax.fori_loop(0, qi // 2, eight_blocks, 0)
        lax.fori_loop(0, qi % 2, four_blocks(8 * (qi // 2)), 0)
        jd = nq * qi
        seen_ab, seen_cd = diagonal_view(0), diagonal_view(2)
        pair_step(jd, pair_ab, pair_cd, seen_ab, lambda g: tuple(r for r, _ in seen_cd(g)))
        pair_step(jd + 2, pair_cd, pair_ab, seen_cd, lambda g: ())

        load_queries(jnp.minimum(qi + 1, n_qblocks - 1))

        q0 = pl.multiple_of(qi * bq, bq)
        for c in range(nq):
            o = normalized(c) - lam * normalized(nq + c)
            on = o * lax.rsqrt(jnp.mean(o * o, axis=0, keepdims=True) + NORM_EPS)
            y = on.T * sg * (1.0 - lambda_init)
            rows = pl.ds(q0 + c * bk, bk)
            o_ref[rows, :] = (ga_ref[rows, :].astype(_F32) * y + gc_ref[rows, :].astype(_F32)).astype(o_ref.dtype)
        reset_state()
        return carry

    load_queries(0)
    reset_state()
    lax.fori_loop(0, n_qblocks, q_block, 0)


def _ffn_kernel(x_ref, mg_ref, wo_ref, wu_ref, wd_ref, par_ref, o_ref,
                h_scr, u_scr, *, tm, n_sub, cf, n_chunks):
    d_ff = cf * n_chunks

    @pl.when(pl.program_id(1) == 0)
    def _():
        for c in range(n_chunks):
            u_scr[c, 0:HALO, :] = jnp.zeros((HALO, 2 * cf), _F32)

    def ab_cols(ref, c):
        return jnp.concatenate([ref[:, c * cf:(c + 1) * cf], ref[:, d_ff + c * cf:d_ff + (c + 1) * cf]], axis=1)

    def up(c):
        return jnp.dot(h_scr[...], ab_cols(wu_ref, c), preferred_element_type=_F32)

    def sub_tile(i, carry):
        rows = pl.ds(pl.multiple_of(i * tm, tm), tm)
        x1 = x_ref[rows, :] + jnp.dot(mg_ref[rows, :], wo_ref[...], preferred_element_type=_F32)
        ms = jnp.mean(x1 * x1, axis=-1, keepdims=True)
        h_scr[...] = (x1 * lax.rsqrt(ms + NORM_EPS) * par_ref[CONV_K + 1:CONV_K + 2, 0:x1.shape[1]]).astype(_BF16)

        acc = x1
        u_next = up(0)
        for c in range(n_chunks):
            u = u_next
            if c + 1 < n_chunks:
                u_next = up(c + 1)
            u_scr[c, HALO:HALO + tm, :] = u
            cw = ab_cols(par_ref, c)
            uc = (cw[2:3, :] * u + cw[1:2, :] * u_scr[c, HALO - 1:HALO - 1 + tm, :]
                  + cw[0:1, :] * u_scr[c, HALO - 2:HALO - 2 + tm, :] + cw[CONV_K:CONV_K + 1, :])
            u_scr[c, 0:HALO, :] = u_scr[c, tm:tm + HALO, :]
            a = uc[:, :cf]
            act = (a * _sigmoid(a) * uc[:, cf:]).astype(_BF16)
            acc = acc + jnp.dot(act, wd_ref[c * cf:(c + 1) * cf, :], preferred_element_type=_F32)
        o_ref[rows, :] = acc
        return carry

    lax.fori_loop(0, n_sub, sub_tile, 0)


def _const_spec(shape):
    nd = len(shape)
    return pl.BlockSpec(shape, lambda *_: (0,) * nd, pipeline_mode=pl.Buffered(1))


def _rope_tables(seq):
    inv = ROPE_THETA ** (-jnp.arange(0, HEAD_DIM, 2, dtype=_F32) / HEAD_DIM)
    ang = jnp.arange(seq, dtype=_F32)[:, None] * inv[None, :]
    c, s = jnp.cos(ang), jnp.sin(ang)
    return jnp.concatenate([c, c, c, c], axis=1), jnp.concatenate([-s, s, -s, s], axis=1)


def kernel(x, attn_norm_g, w_in, b_gate, q_norm_g, k_norm_g, lambda_q1, lambda_k1, lambda_q2, lambda_k2,
           subln_g, short_conv_w, w_out, ffn_norm_g, w_up, ffn_conv_w, ffn_conv_b, w_down):
    bsz, seq, d = x.shape
    depth = attn_norm_g.shape[0]
    d_ff = w_down.shape[1]
    assert d == N_HEADS * V_HEAD_DIM and w_in.shape[2] == 8 * d

    tm1 = min(PROJ_ROWS, seq)
    n_sub1 = PROJ_SUBTILES if seq % (PROJ_SUBTILES * tm1) == 0 else 1
    bk = MXU_TILE
    nq = QUERY_PARTS
    bq = nq * bk
    tm3 = min(FFN_ROWS, seq)
    n_sub3 = FFN_SUBTILES if seq % (FFN_SUBTILES * tm3) == 0 else 1
    cf = MXU_TILE
    n_chunks = d_ff // cf
    assert seq % tm1 == 0 and seq % bq == 0 and seq % tm3 == 0 and d_ff % cf == 0 and tm1 % bk == 0

    cosp, sinp = _rope_tables(seq)
    seg = jnp.kron(jnp.eye(MXU_TILE // HEAD_DIM, dtype=_F32), jnp.full((HEAD_DIM, HEAD_DIM), 1.0 / HEAD_DIM, _F32)).astype(_BF16)
    params = pltpu.CompilerParams(dimension_semantics=("arbitrary", "arbitrary"), vmem_limit_bytes=VMEM_LIMIT_BYTES)

    for l in range(depth):
        lambda_init = 0.8 - 0.6 * math.exp(-0.3 * l)
        qg = jnp.tile(q_norm_g[l] * (HEAD_DIM ** -0.5 * math.log2(math.e)), 2 * N_HEADS)[None, :]
        kg = jnp.tile(k_norm_g[l], 2 * N_HEADS)[None, :]

        proj_par = jnp.zeros((HALO, d), _F32).at[0].set(attn_norm_g[l]).at[1].set(qg[0]).at[2].set(kg[0])
        proj_par = proj_par.at[3].set(b_gate[l, :d]).at[4].set(b_gate[l, d:])
        act = jax.ShapeDtypeStruct((bsz, seq, d), _BF16)
        blk1 = n_sub1 * tm1
        tok1 = pl.BlockSpec((None, blk1, d), lambda b, i: (b, i, 0))
        q, k, vt, ga, gc = pl.pallas_call(
            functools.partial(_proj_kernel, tm=tm1, n_sub=n_sub1, bk=bk, d=d),
            grid=(bsz, seq // blk1),
            in_specs=[tok1, _const_spec((d, 8 * d)),
                      pl.BlockSpec((blk1, LANES), lambda b, i: (i, 0)), pl.BlockSpec((blk1, LANES), lambda b, i: (i, 0)),
                      _const_spec((MXU_TILE, MXU_TILE)), _const_spec((CONV_K, d)), _const_spec((HALO, d))],
            out_specs=[tok1, tok1, pl.BlockSpec((None, blk1 // bk, d, bk), lambda b, i: (b, i, 0, 0)), tok1, tok1],
            out_shape=[act, act, jax.ShapeDtypeStruct((bsz, seq // bk, d, bk), _BF16), act, act],
            scratch_shapes=[pltpu.VMEM((tm1 + HALO, d), _F32)],
            compiler_params=params, name="norm_proj",
        )(x, w_in[l].astype(_BF16), cosp, sinp, seg, short_conv_w[l], proj_par)

        lam4 = jnp.stack([lambda_q1[l], lambda_k1[l], lambda_q2[l], lambda_k2[l]]).astype(_F32)
        attn_par = jnp.zeros((PARAM_ROWS, LANES), _F32).at[0:4, 0:HEAD_DIM].set(lam4).at[PARAM_ROWS // 2, :].set(subln_g[l])
        head = pl.BlockSpec((None, seq, V_HEAD_DIM), lambda b, hd: (b, 0, hd))
        merged = pl.pallas_call(
            functools.partial(_attn_kernel, seq=seq, bk=bk, nq=nq, lambda_init=lambda_init),
            grid=(bsz, N_HEADS),
            in_specs=[_const_spec((PARAM_ROWS, LANES)), head, head,
                      pl.BlockSpec((None, seq // bk, V_HEAD_DIM, bk), lambda b, hd: (b, 0, hd, 0)),
                      head, head],
            out_specs=head,
            out_shape=act,
            scratch_shapes=[pltpu.VMEM((2 * nq, V_HEAD_DIM, bk), _BF16),
                            *[pltpu.VMEM((2 * nq, bk, bk), _F32)] * 4, *[pltpu.VMEM((2 * nq, 1, bk), _F32)] * 4,
                            pltpu.VMEM((2 * nq, V_HEAD_DIM + SUM_ROWS, bk), _F32), pltpu.VMEM((2 * nq, 1, bk), _F32)],
            compiler_params=params, name="diff_attn",
        )(attn_par, q, k, vt, ga, gc)

        ffn_par = jnp.zeros((HALO, 2 * d_ff), _F32).at[0:CONV_K].set(ffn_conv_w[l]).at[CONV_K].set(ffn_conv_b[l])
        ffn_par = ffn_par.at[CONV_K + 1, 0:d].set(ffn_norm_g[l])
        tok3 = pl.BlockSpec((None, n_sub3 * tm3, d), lambda b, i: (b, i, 0))
        x = pl.pallas_call(
            functools.partial(_ffn_kernel, tm=tm3, n_sub=n_sub3, cf=cf, n_chunks=n_chunks),
            grid=(bsz, seq // (n_sub3 * tm3)),
            in_specs=[tok3, tok3, _const_spec((d, d)), _const_spec((d, 2 * d_ff)), _const_spec((d_ff, d)),
                      _const_spec((HALO, 2 * d_ff))],
            out_specs=tok3,
            out_shape=jax.ShapeDtypeStruct((bsz, seq, d), x.dtype),
            scratch_shapes=[pltpu.VMEM((tm3, d), _BF16), pltpu.VMEM((n_chunks, tm3 + HALO, 2 * cf), _F32)],
            compiler_params=params, name="out_proj_ffn",
        )(x, merged, w_out[l].astype(_BF16), w_up[l].astype(_BF16), w_down[l].astype(_BF16), ffn_par)
    return x
```
